```python
import math, functools
import jax, jax.numpy as jnp
from jax import lax
import numpy as np

D_MODEL = 2048
BATCH = 8
SEQ = 2048
DEPTH = 1
DEC_BATCH = 128
DEC_SEQ = 8
PAST_LEN = 16384
PAGE_SIZE = 128

PLE_DIM = 256
D_FF = 5632
NORM_EPS = 1e-6
MLA_HEADS = 16
Q_LORA = 512
KV_LORA = 512
NOPE_DIM = 128
ROPE_DIM = 64
V_DIM = D_MODEL // MLA_HEADS
MLA_QK_DIM = NOPE_DIM + ROPE_DIM
MLA_SCALE = MLA_QK_DIM ** -0.5
ROPE_BASE = 10000.0
DIFF_HEADS = 8
DIFF_HEAD_DIM = D_MODEL // (2 * DIFF_HEADS)
DIFF_SCALE = DIFF_HEAD_DIM ** -0.5
Q_BLOCK = 128
COLS_MLA_Q = Q_LORA
COLS_MLA_KV = KV_LORA + ROPE_DIM
COLS_DIFF_Q = DIFF_HEADS * 2 * DIFF_HEAD_DIM
COLS_DIFF_K = 2 * DIFF_HEAD_DIM
COLS_DIFF_V = 2 * DIFF_HEAD_DIM
COLS_GATE = 2 * D_MODEL
IN_SPLITS = (COLS_MLA_Q, COLS_MLA_Q + COLS_MLA_KV, COLS_MLA_Q + COLS_MLA_KV + COLS_DIFF_Q,
             COLS_MLA_Q + COLS_MLA_KV + COLS_DIFF_Q + COLS_DIFF_K,
             COLS_MLA_Q + COLS_MLA_KV + COLS_DIFF_Q + COLS_DIFF_K + COLS_DIFF_V)
W_IN_COLS = IN_SPLITS[-1] + COLS_GATE

kernel_name = 'mla_diffattn_gated_hybrid_step'


def rmsnorm(x, g):
    xf = x.astype(jnp.float32)
    y = xf * lax.rsqrt(jnp.mean(xf * xf, axis=-1, keepdims=True) + NORM_EPS)
    return (y * g.astype(jnp.float32)).astype(x.dtype)


def swiglu(x, wg, wu, wd):
    return (jax.nn.silu(x @ wg) * (x @ wu)) @ wd


def rope_tables(pos):
    inv = 1.0 / (ROPE_BASE ** (jnp.arange(0, ROPE_DIM, 2, dtype=jnp.float32) / ROPE_DIM))
    ang = pos.astype(jnp.float32)[:, None] * inv[None, :]
    return jnp.cos(ang), jnp.sin(ang)


def rope(x, cos, sin):
    xf = x.astype(jnp.float32)
    x1, x2 = jnp.split(xf, 2, axis=-1)
    return jnp.concatenate([x1 * cos - x2 * sin, x1 * sin + x2 * cos], axis=-1).astype(x.dtype)


def alibi_slopes():
    return jnp.exp2(-(8.0 / DIFF_HEADS) * jnp.arange(1, DIFF_HEADS + 1, dtype=jnp.float32))


def mla_attend(q, kv, qpos, kpos):
    s = jnp.einsum('thc,sc->hts', q, kv).astype(jnp.float32) * MLA_SCALE
    mask = kpos[None, :] <= qpos[:, None]
    s = jnp.where(mask[None], s, -jnp.inf)
    p = jax.nn.softmax(s, axis=-1).astype(kv.dtype)
    return jnp.einsum('hts,sc->thc', p, kv[:, :KV_LORA])


def diff_attend(q, k, v, lam, slopes, qpos, kpos):
    k2 = k.reshape(k.shape[0], 2, DIFF_HEAD_DIM)
    s = jnp.einsum('thmd,smd->hmts', q, k2).astype(jnp.float32) * DIFF_SCALE
    dist = (qpos[:, None] - kpos[None, :]).astype(jnp.float32)
    s = jnp.where((dist >= 0)[None, None], s - slopes[:, None, None, None] * dist, -jnp.inf)
    a = jax.nn.softmax(s, axis=-1)
    w = (a[:, 0] - lam * a[:, 1]).astype(v.dtype)
    return jnp.einsum('hts,se->the', w, v)


def prompt_attention(slopes, q_mla, kv_mla, q_diff, k_diff, v_diff, lam):
    b, s = q_mla.shape[0], q_mla.shape[1]
    nb = s // Q_BLOCK
    kpos = jnp.arange(s)

    def to_blocks(a):
        return jnp.swapaxes(a.reshape(b, nb, Q_BLOCK, *a.shape[2:]), 0, 1)

    def from_blocks(a):
        return jnp.swapaxes(a, 0, 1).reshape(b, s, *a.shape[3:])

    def block(args):
        idx, qm, qd = args
        qpos = idx * Q_BLOCK + jnp.arange(Q_BLOCK)
        lat = jax.vmap(mla_attend, in_axes=(0, 0, None, None))(qm, kv_mla, qpos, kpos)
        dout = jax.vmap(diff_attend, in_axes=(0, 0, 0, None, None, None, None))(
            qd, k_diff, v_diff, lam, slopes, qpos, kpos)
        return lat, dout

    lat, dout = lax.map(block, (jnp.arange(nb), to_blocks(q_mla), to_blocks(q_diff)))
    return from_blocks(lat), from_blocks(dout)


def sample_attention(slopes, cache_mla, cache_k, cache_v, page_table, layer_idx,
                     q_mla, kv_new, q_diff, k_new, v_new, lam):
    t = q_mla.shape[1]
    past = page_table.shape[1] * cache_mla.shape[2]
    qpos = past + jnp.arange(t)
    kpos = jnp.arange(past + t)

    def one(args):
        pt, qm, kvn, qd, kn, vn = args

        def gather(cache, new):
            rows = cache[layer_idx, pt]
            return jnp.concatenate([rows.reshape(past, rows.shape[-1]).astype(new.dtype), new], axis=0)

        lat = mla_attend(qm, gather(cache_mla, kvn), qpos, kpos)
        dout = diff_attend(qd, gather(cache_k, kn), gather(cache_v, vn), lam, slopes, qpos, kpos)
        return lat, dout

    return lax.map(one, (page_table, q_mla, kv_new, q_diff, k_new, v_new))


def layer(x, p, cos, sin, w, attend, lam_init):
    lead = x.shape[:-1]
    h = x + 0.5 * rmsnorm(swiglu(rmsnorm(x, w['ffn1_pre_g']), w['ffn1_wg'], w['ffn1_wu'], w['ffn1_wd']),
                          w['ffn1_post_g'])
    u = rmsnorm(h, w['mix_pre_g'])
    cq, ckv, dq, dk, dv, gl = jnp.split(u @ w['w_in'], IN_SPLITS, axis=-1)
    q = (rmsnorm(cq, w['q_a_norm_g']) @ w['w_uq']).reshape(*lead, MLA_HEADS, MLA_QK_DIM)
    q_lat = jnp.einsum('bthd,chd->bthc', q[..., :NOPE_DIM], w['w_uk'])
    q_rot = rope(q[..., NOPE_DIM:], cos[:, None, :], sin[:, None, :])
    q_mla = jnp.concatenate([q_lat, q_rot], axis=-1)
    kv_mla = jnp.concatenate([rmsnorm(ckv[..., :KV_LORA], w['kv_a_norm_g']),
                              rope(ckv[..., KV_LORA:], cos, sin)], axis=-1)
    q_diff = dq.reshape(*lead, DIFF_HEADS, 2, DIFF_HEAD_DIM)
    f32 = jnp.float32
    lam = (jnp.exp(jnp.sum(w['lambda_q1'].astype(f32) * w['lambda_k1'].astype(f32)))
           - jnp.exp(jnp.sum(w['lambda_q2'].astype(f32) * w['lambda_k2'].astype(f32))) + lam_init)
    lat, dout = attend(q_mla, kv_mla, q_diff, dk, dv, lam)
    o_mla = jnp.einsum('bthc,chv->bthv', lat, w['w_uv']).reshape(*lead, D_MODEL)
    o_diff = (rmsnorm(dout, w['diff_subln_g']) * (1.0 - lam_init)).reshape(*lead, D_MODEL)
    gate = jax.nn.sigmoid(gl).reshape(*lead, 2, D_MODEL)
    merged = gate[..., 0, :] * o_mla + gate[..., 1, :] * o_diff
    h = h + rmsnorm(merged @ w['w_o'], w['mix_post_g'])
    h = h + 0.5 * rmsnorm(swiglu(rmsnorm(h, w['ffn2_pre_g']), w['ffn2_wg'], w['ffn2_wu'], w['ffn2_wd']),
                          w['ffn2_post_g'])
    h = h + rmsnorm(jax.nn.sigmoid(h @ w['w_ple_gate']) * (p @ w['w_ple']), w['ple_post_g'])
    return h, kv_mla, dk, dv


def setup_inputs(seed: int = 0) -> dict:
    key = jax.random.key(seed)
    keys = iter(jax.random.split(key, 48))

    def normal(shape, scale):
        return jax.random.normal(next(keys), shape, jnp.float32) * scale

    def gain(n):
        return 1.0 + normal((DEPTH, n), 0.05)

    n_pages = PAST_LEN // PAGE_SIZE
    n_phys = (5 * DEC_BATCH * n_pages) // 4
    L = DEPTH
    x_prompt = normal((BATCH, SEQ, D_MODEL), 1.0)
    x_sample = normal((DEC_BATCH, DEC_SEQ, D_MODEL), 1.0)
    cache_mla = normal((L, n_phys, PAGE_SIZE, KV_LORA + ROPE_DIM), 1.0)
    cache_diff_k = normal((L, n_phys, PAGE_SIZE, 2 * DIFF_HEAD_DIM), 1.0)
    cache_diff_v = normal((L, n_phys, PAGE_SIZE, 2 * DIFF_HEAD_DIM), 1.0)
    page_table = jax.random.permutation(next(keys), n_phys)[: DEC_BATCH * n_pages].reshape(
        DEC_BATCH, n_pages).astype(jnp.int32)
    p_prompt = normal((L, BATCH, SEQ, PLE_DIM), 1.0)
    p_sample = normal((L, DEC_BATCH, DEC_SEQ, PLE_DIM), 1.0)
    dm = D_MODEL ** -0.5
    return {
        'x_prompt': x_prompt, 'x_sample': x_sample,
        'cache_mla': cache_mla, 'cache_diff_k': cache_diff_k, 'cache_diff_v': cache_diff_v,
        'page_table': page_table, 'p_prompt': p_prompt, 'p_sample': p_sample,
        'ffn1_pre_g': gain(D_MODEL),
        'ffn1_wg': normal((L, D_MODEL, D_FF), dm), 'ffn1_wu': normal((L, D_MODEL, D_FF), dm),
        'ffn1_wd': normal((L, D_FF, D_MODEL), D_FF ** -0.5), 'ffn1_post_g': gain(D_MODEL),
        'mix_pre_g': gain(D_MODEL), 'w_in': normal((L, D_MODEL, W_IN_COLS), dm),
        'q_a_norm_g': gain(Q_LORA), 'w_uq': normal((L, Q_LORA, MLA_HEADS * MLA_QK_DIM), Q_LORA ** -0.5),
        'kv_a_norm_g': gain(KV_LORA),
        'w_uk': normal((L, KV_LORA, MLA_HEADS, NOPE_DIM), KV_LORA ** -0.5),
        'w_uv': normal((L, KV_LORA, MLA_HEADS, V_DIM), KV_LORA ** -0.5),
        'lambda_q1': normal((L, DIFF_HEAD_DIM), 0.1), 'lambda_k1': normal((L, DIFF_HEAD_DIM), 0.1),
        'lambda_q2': normal((L, DIFF_HEAD_DIM), 0.1), 'lambda_k2': normal((L, DIFF_HEAD_DIM), 0.1),
        'diff_subln_g': gain(2 * DIFF_HEAD_DIM),
        'w_o': normal((L, D_MODEL, D_MODEL), dm), 'mix_post_g': gain(D_MODEL),
        'ffn2_pre_g': gain(D_MODEL),
        'ffn2_wg': normal((L, D_MODEL, D_FF), dm), 'ffn2_wu': normal((L, D_MODEL, D_FF), dm),
        'ffn2_wd': normal((L, D_FF, D_MODEL), D_FF ** -0.5), 'ffn2_post_g': gain(D_MODEL),
        'w_ple': normal((L, PLE_DIM, D_MODEL), PLE_DIM ** -0.5),
        'w_ple_gate': normal((L, D_MODEL, D_MODEL), dm), 'ple_post_g': gain(D_MODEL),
    }


def reference(x_prompt, x_sample, cache_mla, cache_diff_k, cache_diff_v, page_table, p_prompt, p_sample,
              ffn1_pre_g, ffn1_wg, ffn1_wu, ffn1_wd, ffn1_post_g, mix_pre_g, w_in, q_a_norm_g, w_uq,
              kv_a_norm_g, w_uk, w_uv, lambda_q1, lambda_k1, lambda_q2, lambda_k2, diff_subln_g, w_o,
              mix_post_g, ffn2_pre_g, ffn2_wg, ffn2_wu, ffn2_wd, ffn2_post_g, w_ple, w_ple_gate, ple_post_g):
    slopes = alibi_slopes()
    cos_p, sin_p = rope_tables(jnp.arange(x_prompt.shape[1]))
    cos_s, sin_s = rope_tables(PAST_LEN + jnp.arange(x_sample.shape[1]))
    hp, hs = x_prompt, x_sample
    mla_p, dk_p, dv_p, mla_s, dk_s, dv_s = [], [], [], [], [], []
    for i in range(DEPTH):
        w = {
            'ffn1_pre_g': ffn1_pre_g[i], 'ffn1_wg': ffn1_wg[i], 'ffn1_wu': ffn1_wu[i], 'ffn1_wd': ffn1_wd[i],
            'ffn1_post_g': ffn1_post_g[i], 'mix_pre_g': mix_pre_g[i], 'w_in': w_in[i],
            'q_a_norm_g': q_a_norm_g[i], 'w_uq': w_uq[i], 'kv_a_norm_g': kv_a_norm_g[i],
            'w_uk': w_uk[i], 'w_uv': w_uv[i], 'lambda_q1': lambda_q1[i], 'lambda_k1': lambda_k1[i],
            'lambda_q2': lambda_q2[i], 'lambda_k2': lambda_k2[i], 'diff_subln_g': diff_subln_g[i],
            'w_o': w_o[i], 'mix_post_g': mix_post_g[i], 'ffn2_pre_g': ffn2_pre_g[i],
            'ffn2_wg': ffn2_wg[i], 'ffn2_wu': ffn2_wu[i], 'ffn2_wd': ffn2_wd[i],
            'ffn2_post_g': ffn2_post_g[i], 'w_ple': w_ple[i], 'w_ple_gate': w_ple_gate[i],
            'ple_post_g': ple_post_g[i],
        }
        lam_init = 0.8 - 0.6 * math.exp(-0.3 * i)
        attend_p = functools.partial(prompt_attention, slopes)
        attend_s = functools.partial(sample_attention, slopes, cache_mla, cache_diff_k, cache_diff_v,
                                     page_table, i)
        hp, kvp, kp, vp = layer(hp, p_prompt[i], cos_p, sin_p, w, attend_p, lam_init)
        hs, kvs, ks, vs = layer(hs, p_sample[i], cos_s, sin_s, w, attend_s, lam_init)
        mla_p.append(kvp); dk_p.append(kp); dv_p.append(vp)
        mla_s.append(kvs); dk_s.append(ks); dv_s.append(vs)
    return (hp, hs, jnp.stack(mla_p), jnp.stack(dk_p), jnp.stack(dv_p),
            jnp.stack(mla_s), jnp.stack(dk_s), jnp.stack(dv_s))
```

```python
import functools
import math

import jax
import jax.numpy as jnp
from jax import lax
from jax.experimental import pallas as pl
from jax.experimental.pallas import tpu as pltpu

F32 = jnp.float32
BF16 = jnp.bfloat16

NORM_EPS = 1e-6
MLA_HEADS = 16
NOPE_DIM = 128
ROPE_DIM = 64
KV_LORA = 512
Q_LORA = 512
V_DIM = 128
MLA_SCALE = (NOPE_DIM + ROPE_DIM) ** -0.5
ROPE_BASE = 10000.0
DIFF_HEADS = 8
DIFF_HEAD_DIM = 128
DIFF_SCALE = DIFF_HEAD_DIM ** -0.5
LAM_INIT = 0.8 - 0.6 * math.exp(-0.3 * 0)

COL_GATE = 0
COL_DQ = 4096
COL_CQ = 6144
COL_LAT = 6656
COL_DK = 7168
COL_DV = 7424
COL_ROPE = 7680
PROJ_COLS = 8192

VMEM_LIMIT = 52 * 1024 * 1024


def _params(*sem):
    return pltpu.CompilerParams(dimension_semantics=sem, vmem_limit_bytes=VMEM_LIMIT)


def _rms(x, g):
    return x * lax.rsqrt(jnp.mean(x * x, axis=-1, keepdims=True) + NORM_EPS) * g


def _dot(a, b):
    return jnp.dot(a, b, preferred_element_type=F32)


def _dot_nt(a, b):
    return lax.dot_general(a, b, (((1,), (1,)), ((), ())), preferred_element_type=F32)


def _ffn_kernel(x_ref, gpre_ref, wg_ref, wu_ref, wd_ref, gpost_ref, o_ref, xn_ref, acc_ref):
    j = pl.program_id(1)

    @pl.when(j == 0)
    def _():
        xn_ref[...] = _rms(x_ref[...], gpre_ref[...]).astype(BF16)
        acc_ref[...] = jnp.zeros_like(acc_ref)

    xn = xn_ref[...]
    g = _dot(xn, wg_ref[...])
    u = _dot(xn, wu_ref[...])
    mid = (g * jax.nn.sigmoid(g) * u).astype(BF16)
    acc_ref[...] += _dot(mid, wd_ref[...])

    @pl.when(j == pl.num_programs(1) - 1)
    def _():
        o_ref[...] = x_ref[...] + 0.5 * _rms(acc_ref[...], gpost_ref[...])


def _ffn(x, gpre, wg, wu, wd, gpost, *, tm=512, tf=512):
    n, d = x.shape
    dff = wg.shape[1]
    return pl.pallas_call(
        _ffn_kernel,
        grid=(n // tm, dff // tf),
        in_specs=[
            pl.BlockSpec((tm, d), lambda i, j: (i, 0)),
            pl.BlockSpec((1, d), lambda i, j: (0, 0)),
            pl.BlockSpec((d, tf), lambda i, j: (0, j)),
            pl.BlockSpec((d, tf), lambda i, j: (0, j)),
            pl.BlockSpec((tf, d), lambda i, j: (j, 0)),
            pl.BlockSpec((1, d), lambda i, j: (0, 0)),
        ],
        out_specs=pl.BlockSpec((tm, d), lambda i, j: (i, 0)),
        out_shape=jax.ShapeDtypeStruct((n, d), F32),
        scratch_shapes=[pltpu.VMEM((tm, d), BF16), pltpu.VMEM((tm, d), F32)],
        compiler_params=_params("parallel", "arbitrary"),
        name="ffn",
    )(x, gpre, wg, wu, wd, gpost)


def _inproj_kernel(h_ref, g_ref, w_ref, o_ref, u_ref):
    @pl.when(pl.program_id(1) == 0)
    def _():
        u_ref[...] = _rms(h_ref[...], g_ref[...]).astype(BF16)

    o_ref[...] = _dot(u_ref[...], w_ref[...])


def _inproj(h, g, w, *, tm=1024, tn=512):
    n, d = h.shape
    cols = w.shape[1]
    return pl.pallas_call(
        _inproj_kernel,
        grid=(n // tm, cols // tn),
        in_specs=[
            pl.BlockSpec((tm, d), lambda i, j: (i, 0)),
            pl.BlockSpec((1, d), lambda i, j: (0, 0)),
            pl.BlockSpec((d, tn), lambda i, j: (0, j)),
        ],
        out_specs=pl.BlockSpec((tm, tn), lambda i, j: (i, j)),
        out_shape=jax.ShapeDtypeStruct((n, cols), F32),
        scratch_shapes=[pltpu.VMEM((tm, d), BF16)],
        compiler_params=_params("parallel", "arbitrary"),
        name="inproj",
    )(h, g, w)


def _rope_tile(t, c, s):
    return t * c + pltpu.roll(t, 64, 1) * s


def _kv_kernel(lat_ref, rope_ref, c_ref, s_ref, g_ref, wuk_ref, wuv_ref, kv_ref, k_ref, v_ref):
    latn = _rms(lat_ref[...], g_ref[...])
    rot = _rope_tile(rope_ref[...], c_ref[...], s_ref[...])
    kv_ref[:, :KV_LORA] = latn
    kv_ref[:, KV_LORA:] = rot[:, :ROPE_DIM]
    lb = latn.astype(BF16)
    kn = _dot(lb, wuk_ref[...])
    v_ref[...] = _dot(lb, wuv_ref[...]).astype(BF16)
    rotb = rot.astype(BF16)
    for h in range(MLA_HEADS):
        k_ref[:, h * 256:h * 256 + 128] = kn[:, h * 128:(h + 1) * 128].astype(BF16)
        k_ref[:, h * 256 + 128:(h + 1) * 256] = rotb


def _kv_side(proj, cos_t, sin_t, g, wuk, wuv, *, tm=256):
    n = proj.shape[0]
    hd = MLA_HEADS * V_DIM
    return pl.pallas_call(
        _kv_kernel,
        grid=(n // tm,),
        in_specs=[
            pl.BlockSpec((tm, KV_LORA), lambda i: (i, COL_LAT // KV_LORA)),
            pl.BlockSpec((tm, 128), lambda i: (i, COL_ROPE // 128)),
            pl.BlockSpec((tm, 128), lambda i: (i, 0)),
            pl.BlockSpec((tm, 128), lambda i: (i, 0)),
            pl.BlockSpec((1, KV_LORA), lambda i: (0, 0)),
            pl.BlockSpec((KV_LORA, hd), lambda i: (0, 0)),
            pl.BlockSpec((KV_LORA, hd), lambda i: (0, 0)),
        ],
        out_specs=[
            pl.BlockSpec((tm, KV_LORA + ROPE_DIM), lambda i: (i, 0)),
            pl.BlockSpec((tm, 2 * hd), lambda i: (i, 0)),
            pl.BlockSpec((tm, hd), lambda i: (i, 0)),
        ],
        out_shape=[
            jax.ShapeDtypeStruct((n, KV_LORA + ROPE_DIM), F32),
            jax.ShapeDtypeStruct((n, 2 * hd), BF16),
            jax.ShapeDtypeStruct((n, hd), BF16),
        ],
        compiler_params=_params("parallel"),
        name="kv_side",
    )(proj, proj, cos_t, sin_t, g, wuk, wuv)


def _q_kernel(cq_ref, c_ref, s_ref, g_ref, wq_ref, q_ref):
    cqn = _rms(cq_ref[...], g_ref[...]).astype(BF16)
    qa = _dot(cqn, wq_ref[...])
    c = c_ref[...]
    s = s_ref[...]
    for h in range(MLA_HEADS):
        q_ref[:, h * 256:h * 256 + 128] = qa[:, h * 256:h * 256 + 128].astype(BF16)
        t = qa[:, h * 256 + 128:(h + 1) * 256]
        q_ref[:, h * 256 + 128:(h + 1) * 256] = _rope_tile(t, c, s).astype(BF16)


def _q_side(proj, cos_t, sin_t, g, wq, *, tm=256):
    n = proj.shape[0]
    cols = wq.shape[1]
    return pl.pallas_call(
        _q_kernel,
        grid=(n // tm,),
        in_specs=[
            pl.BlockSpec((tm, Q_LORA), lambda i: (i, COL_CQ // Q_LORA)),
            pl.BlockSpec((tm, 128), lambda i: (i, 0)),
            pl.BlockSpec((tm, 128), lambda i: (i, 0)),
            pl.BlockSpec((1, Q_LORA), lambda i: (0, 0)),
            pl.BlockSpec((Q_LORA, cols), lambda i: (0, 0)),
        ],
        out_specs=pl.BlockSpec((tm, cols), lambda i: (i, 0)),
        out_shape=jax.ShapeDtypeStruct((n, cols), BF16),
        compiler_params=_params("parallel"),
        name="q_side",
    )(proj, cos_t, sin_t, g, wq)


def _qlat_kernel(q_ref, w_ref, o_ref):
    o_ref[...] = _dot(q_ref[:, :NOPE_DIM], w_ref[...]).astype(BF16)


def _q_absorb(q, wukt, row0, n_s, *, tm=256):
    return pl.pallas_call(
        _qlat_kernel,
        grid=(n_s // tm, MLA_HEADS),
        in_specs=[
            pl.BlockSpec((tm, 256), lambda i, h: (row0 // tm + i, h)),
            pl.BlockSpec((None, NOPE_DIM, KV_LORA), lambda i, h: (h, 0, 0)),
        ],
        out_specs=pl.BlockSpec((tm, KV_LORA), lambda i, h: (i, h)),
        out_shape=jax.ShapeDtypeStruct((n_s, MLA_HEADS * KV_LORA), BF16),
        compiler_params=_params("parallel", "arbitrary"),
        name="q_absorb",
    )(q, wukt)


def _uv_kernel(lat_ref, w_ref, o_ref):
    o_ref[...] = _dot(lat_ref[...].astype(BF16), w_ref[...])


def _v_up(lat, wuvh, *, tm=256):
    n_s = lat.shape[0]
    return pl.pallas_call(
        _uv_kernel,
        grid=(n_s // tm, MLA_HEADS),
        in_specs=[
            pl.BlockSpec((tm, KV_LORA), lambda i, h: (i, h)),
            pl.BlockSpec((None, KV_LORA, V_DIM), lambda i, h: (h, 0, 0)),
        ],
        out_specs=pl.BlockSpec((tm, V_DIM), lambda i, h: (i, h)),
        out_shape=jax.ShapeDtypeStruct((n_s, MLA_HEADS * V_DIM), F32),
        compiler_params=_params("parallel", "arbitrary"),
        name="v_up",
    )(lat, wuvh)


def _softmax_step(s, v, m, l, acc):
    m_new = jnp.maximum(m, jnp.max(s, axis=-1, keepdims=True))
    alpha = jnp.exp(m - m_new)
    p = jnp.exp(s - m_new)
    l_new = alpha * l + jnp.sum(p, axis=-1, keepdims=True)
    acc_new = alpha * acc + _dot(p.astype(BF16), v)
    return m_new, l_new, acc_new


def _pmla_kernel(q_ref, k_ref, v_ref, o_ref, *, tq):
    qi = pl.program_id(2)
    q = q_ref[...]

    def step(kb, carry, masked):
        off = pl.multiple_of(kb * tq, tq)
        k = k_ref[pl.ds(off, tq), :]
        v = v_ref[pl.ds(off, tq), :]
        s = _dot_nt(q, k) * MLA_SCALE
        if masked:
            row = lax.broadcasted_iota(jnp.int32, (tq, tq), 0)
            col = lax.broadcasted_iota(jnp.int32, (tq, tq), 1)
            s = jnp.where(col <= row, s, -jnp.inf)
        return _softmax_step(s, v, *carry)

    init = (jnp.full((tq, 1), -jnp.inf, F32), jnp.zeros((tq, 1), F32), jnp.zeros((tq, V_DIM), F32))
    carry = lax.fori_loop(0, qi, lambda kb, c: step(kb, c, False), init)
    m, l, acc = step(qi, carry, True)
    o_ref[...] = acc / l


def _prompt_mla(q, k, v, batch, seq, *, tq=512):
    nq = seq // tq
    return pl.pallas_call(
        functools.partial(_pmla_kernel, tq=tq),
        grid=(batch, MLA_HEADS, nq),
        in_specs=[
            pl.BlockSpec((tq, 256), lambda b, h, i: (b * nq + i, h)),
            pl.BlockSpec((seq, 256), lambda b, h, i: (b, h)),
            pl.BlockSpec((seq, V_DIM), lambda b, h, i: (b, h)),
        ],
        out_specs=pl.BlockSpec((tq, V_DIM), lambda b, h, i: (b * nq + i, h)),
        out_shape=jax.ShapeDtypeStruct((batch * seq, MLA_HEADS * V_DIM), F32),
        compiler_params=_params("parallel", "parallel", "arbitrary"),
        name="prompt_mla",
    )(q, k, v)


def _lambda(lq1_ref, lk1_ref, lq2_ref, lk2_ref):
    a = jnp.sum(lq1_ref[...] * lk1_ref[...], axis=-1, keepdims=True)
    b = jnp.sum(lq2_ref[...] * lk2_ref[...], axis=-1, keepdims=True)
    return jnp.exp(a) - jnp.exp(b) + LAM_INIT


def _diff_finish(a1, l1, a2, l2, lam, g):
    out = a1 / l1 - lam * (a2 / l2)
    return _rms(out, g) * (1.0 - LAM_INIT)


def _pdiff_kernel(q_ref, k_ref, v_ref, lq1_ref, lk1_ref, lq2_ref, lk2_ref, g_ref, o_ref, kb_ref, vb_ref, *, tq):
    h = pl.program_id(1)
    qi = pl.program_id(2)

    @pl.when((h == 0) & (qi == 0))
    def _():
        kb_ref[...] = k_ref[...].astype(BF16)
        vb_ref[...] = v_ref[...].astype(BF16)

    q = q_ref[...]
    q1 = q[:, :DIFF_HEAD_DIM].astype(BF16)
    q2 = q[:, DIFF_HEAD_DIM:].astype(BF16)
    slope = jnp.exp2(-jnp.full((1, 1), h + 1, jnp.int32).astype(F32) * (8.0 / DIFF_HEADS))
    row = lax.broadcasted_iota(jnp.int32, (tq, tq), 0)
    col = lax.broadcasted_iota(jnp.int32, (tq, tq), 1)
    rel = row - col

    def step(kb, carry, masked):
        off = pl.multiple_of(kb * tq, tq)
        k = kb_ref[pl.ds(off, tq), :]
        v = vb_ref[pl.ds(off, tq), :]
        dist = (rel + (qi - kb) * tq).astype(F32)
        bias = slope * dist
        s1 = _dot_nt(q1, k[:, :DIFF_HEAD_DIM]) * DIFF_SCALE - bias
        s2 = _dot_nt(q2, k[:, DIFF_HEAD_DIM:]) * DIFF_SCALE - bias
        if masked:
            s1 = jnp.where(dist >= 0, s1, -jnp.inf)
            s2 = jnp.where(dist >= 0, s2, -jnp.inf)
        c1, c2 = carry
        return _softmax_step(s1, v, *c1), _softmax_step(s2, v, *c2)

    def init():
        return (jnp.full((tq, 1), -jnp.inf, F32), jnp.zeros((tq, 1), F32),
                jnp.zeros((tq, 2 * DIFF_HEAD_DIM), F32))

    carry = lax.fori_loop(0, qi, lambda kb, c: step(kb, c, False), (init(), init()))
    (m1, l1, a1), (m2, l2, a2) = step(qi, carry, True)
    lam = _lambda(lq1_ref, lk1_ref, lq2_ref, lk2_ref)
    o_ref[...] = _diff_finish(a1, l1, a2, l2, lam, g_ref[...])


def _prompt_diff(proj, lam_vecs, g, batch, seq, *, tq=512):
    nq = seq // tq
    w = 2 * DIFF_HEAD_DIM
    vec = pl.BlockSpec((1, DIFF_HEAD_DIM), lambda b, h, i: (0, 0))
    return pl.pallas_call(
        functools.partial(_pdiff_kernel, tq=tq),
        grid=(batch, DIFF_HEADS, nq),
        in_specs=[
            pl.BlockSpec((tq, w), lambda b, h, i: (b * nq + i, COL_DQ // w + h)),
            pl.BlockSpec((seq, w), lambda b, h, i: (b, COL_DK // w)),
            pl.BlockSpec((seq, w), lambda b, h, i: (b, COL_DV // w)),
            vec, vec, vec, vec,
            pl.BlockSpec((1, w), lambda b, h, i: (0, 0)),
        ],
        out_specs=pl.BlockSpec((tq, w), lambda b, h, i: (b * nq + i, h)),
        out_shape=jax.ShapeDtypeStruct((batch * seq, DIFF_HEADS * w), F32),
        scratch_shapes=[pltpu.VMEM((seq, w), BF16), pltpu.VMEM((seq, w), BF16)],
        compiler_params=_params("arbitrary", "arbitrary", "arbitrary"),
        name="prompt_diff",
    )(proj, proj, proj, *lam_vecs, g)


def _dmla_kernel(pt_ref, qlat_ref, qrot_ref, new_ref, *rest, pages, page):
    page_refs = rest[:pages]
    o_ref = rest[pages]
    kbuf, m_ref, l_ref, acc_ref = rest[pages + 1:]
    p = pl.program_id(1)
    rows = qlat_ref.shape[0]

    @pl.when(p == 0)
    def _():
        m_ref[...] = jnp.full_like(m_ref, -jnp.inf)
        l_ref[...] = jnp.zeros_like(l_ref)
        acc_ref[...] = jnp.zeros_like(acc_ref)

    qlat = qlat_ref[...]
    qrot = qrot_ref[...]

    def update(k, mask):
        lat = k[:, :KV_LORA]
        s = (_dot_nt(qlat, lat) + _dot_nt(qrot, k[:, KV_LORA:])) * MLA_SCALE
        if mask is not None:
            s = jnp.where(mask, s, -jnp.inf)
        m, l, acc = _softmax_step(s, lat, m_ref[...], l_ref[...], acc_ref[...])
        m_ref[...] = m
        l_ref[...] = l
        acc_ref[...] = acc

    for kk in range(pages):
        kbuf[kk * page:(kk + 1) * page, :] = page_refs[kk][...].astype(BF16)
    update(kbuf[...], None)

    @pl.when(p == pl.num_programs(1) - 1)
    def _():
        t = lax.broadcasted_iota(jnp.int32, (rows, page), 0) // MLA_HEADS
        c = lax.broadcasted_iota(jnp.int32, (rows, page), 1)
        update(new_ref[...].astype(BF16), c <= t)
        o_ref[...] = acc_ref[...] / l_ref[...]


def _decode_mla(page_table, qlat, qrot, kv_new, cache, *, pages=16):
    n_seq, rows, _ = qlat.shape
    n_pages = page_table.shape[1]
    page = cache.shape[2]
    width = cache.shape[3]
    steps = n_pages // pages

    def page_spec(kk):
        return pl.BlockSpec((None, None, page, width),
                            lambda n, p, pt: (0, pt[n * n_pages + p * pages + kk], 0, 0))

    grid_spec = pltpu.PrefetchScalarGridSpec(
        num_scalar_prefetch=1,
        grid=(n_seq, steps),
        in_specs=[
            pl.BlockSpec((None, rows, KV_LORA), lambda n, p, pt: (n, 0, 0)),
            pl.BlockSpec((None, rows, ROPE_DIM), lambda n, p, pt: (n, 0, 0)),
            pl.BlockSpec((None, page, width), lambda n, p, pt: (n, 0, 0)),
        ] + [page_spec(kk) for kk in range(pages)],
        out_specs=pl.BlockSpec((None, rows, KV_LORA), lambda n, p, pt: (n, 0, 0)),
        scratch_shapes=[
            pltpu.VMEM((pages * page, width), BF16),
            pltpu.VMEM((rows, 1), F32),
            pltpu.VMEM((rows, 1), F32),
            pltpu.VMEM((rows, KV_LORA), F32),
        ],
    )
    return pl.pallas_call(
        functools.partial(_dmla_kernel, pages=pages, page=page),
        grid_spec=grid_spec,
        out_shape=jax.ShapeDtypeStruct((n_seq, rows, KV_LORA), F32),
        compiler_params=_params("parallel", "arbitrary"),
        name="decode_mla",
    )(page_table.reshape(-1), qlat, qrot, kv_new, *([cache] * pages))


def _ddiff_kernel(pt_ref, q_ref, knew_ref, vnew_ref, lq1_ref, lk1_ref, lq2_ref, lk2_ref, g_ref, *rest,
                  pages, page, past):
    k_refs = rest[:pages]
    v_refs = rest[pages:2 * pages]
    o_ref = rest[2 * pages]
    kbuf, vbuf, m_ref, l_ref, acc_ref = rest[2 * pages + 1:]
    p = pl.program_id(1)
    rows = q_ref.shape[0]
    half = rows // 2

    @pl.when(p == 0)
    def _():
        m_ref[...] = jnp.full_like(m_ref, -jnp.inf)
        l_ref[...] = jnp.zeros_like(l_ref)
        acc_ref[...] = jnp.zeros_like(acc_ref)

    q = q_ref[...]
    r = lax.broadcasted_iota(jnp.int32, (rows, 1), 0)
    head = r % DIFF_HEADS
    t = (r % half) // DIFF_HEADS
    slope = jnp.exp2(-(head + 1).astype(F32) * (8.0 / DIFF_HEADS))
    qpos = past + t

    def update(k, v, kpos0, mask_new):
        n_keys = k.shape[0]
        c = lax.broadcasted_iota(jnp.int32, (rows, n_keys), 1)
        dist = (qpos - kpos0 - c).astype(F32)
        s = _dot_nt(q, k) * DIFF_SCALE - slope * dist
        if mask_new:
            s = jnp.where(dist >= 0, s, -jnp.inf)
        m, l, acc = _softmax_step(s, v, m_ref[...], l_ref[...], acc_ref[...])
        m_ref[...] = m
        l_ref[...] = l
        acc_ref[...] = acc

    for kk in range(pages):
        kbuf[kk * page:(kk + 1) * page, :] = k_refs[kk][...].astype(BF16)
        vbuf[kk * page:(kk + 1) * page, :] = v_refs[kk][...].astype(BF16)
    update(kbuf[...], vbuf[...], p * (pages * page), False)

    @pl.when(p == pl.num_programs(1) - 1)
    def _():
        update(knew_ref[...].astype(BF16), vnew_ref[...].astype(BF16), past, True)
        acc = acc_ref[...]
        l = l_ref[...]
        lam = _lambda(lq1_ref, lk1_ref, lq2_ref, lk2_ref)
        o_ref[...] = _diff_finish(acc[:half], l[:half], acc[half:], l[half:], lam, g_ref[...])


def _decode_diff(page_table, q_bd, k_new, v_new, lam_vecs, g, cache_k, cache_v, *, pages=16):
    n_seq, rows, w = q_bd.shape
    n_pages = page_table.shape[1]
    page = cache_k.shape[2]
    steps = n_pages // pages
    past = n_pages * page

    def page_spec(kk):
        return pl.BlockSpec((None, None, page, w),
                            lambda n, p, pt: (0, pt[n * n_pages + p * pages + kk], 0, 0))

    vec = pl.BlockSpec((1, DIFF_HEAD_DIM), lambda n, p, pt: (0, 0))
    grid_spec = pltpu.PrefetchScalarGridSpec(
        num_scalar_prefetch=1,
        grid=(n_seq, steps),
        in_specs=[
            pl.BlockSpec((None, rows, w), lambda n, p, pt: (n, 0, 0)),
            pl.BlockSpec((None, page, w), lambda n, p, pt: (n, 0, 0)),
            pl.BlockSpec((None, page, w), lambda n, p, pt: (n, 0, 0)),
            vec, vec, vec, vec,
            pl.BlockSpec((1, w), lambda n, p, pt: (0, 0)),
        ] + [page_spec(kk) for kk in range(pages)] * 2,
        out_specs=pl.BlockSpec((None, rows // 2, w), lambda n, p, pt: (n, 0, 0)),
        scratch_shapes=[
            pltpu.VMEM((pages * page, w), BF16),
            pltpu.VMEM((pages * page, w), BF16),
            pltpu.VMEM((rows, 1), F32),
            pltpu.VMEM((rows, 1), F32),
            pltpu.VMEM((rows, w), F32),
        ],
    )
    return pl.pallas_call(
        functools.partial(_ddiff_kernel, pages=pages, page=page, past=past),
        grid_spec=grid_spec,
        out_shape=jax.ShapeDtypeStruct((n_seq, rows // 2, w), F32),
        compiler_params=_params("parallel", "arbitrary"),
        name="decode_diff",
    )(page_table.reshape(-1), q_bd, k_new, v_new, *lam_vecs, g,
      *([cache_k] * pages), *([cache_v] * pages))


def _merge_kernel(g0_ref, g1_ref, om_ref, od_ref, h_ref, wo_ref, g_ref, o_ref):
    merged = jax.nn.sigmoid(g0_ref[...]) * om_ref[...] + jax.nn.sigmoid(g1_ref[...]) * od_ref[...]
    y = _dot(merged.astype(BF16), wo_ref[...])
    o_ref[...] = h_ref[...] + _rms(y, g_ref[...])


def _merge(proj, o_mla, o_diff, h, wo, g, *, tm=256):
    n, d = h.shape
    row = lambda c: pl.BlockSpec((tm, d), lambda i: (i, c))
    return pl.pallas_call(
        _merge_kernel,
        grid=(n // tm,),
        in_specs=[
            row(COL_GATE // d), row(COL_GATE // d + 1), row(0), row(0), row(0),
            pl.BlockSpec((d, d), lambda i: (0, 0)),
            pl.BlockSpec((1, d), lambda i: (0, 0)),
        ],
        out_specs=row(0),
        out_shape=jax.ShapeDtypeStruct((n, d), F32),
        compiler_params=_params("parallel"),
        name="merge",
    )(proj, proj, o_mla, o_diff, h, wo, g)


def _ple_kernel(h_ref, p_ref, wg_ref, wp_ref, g_ref, o_ref):
    h = h_ref[...]
    gate = jax.nn.sigmoid(_dot(h.astype(BF16), wg_ref[...]))
    emb = _dot(p_ref[...].astype(BF16), wp_ref[...])
    o_ref[...] = h + _rms(gate * emb, g_ref[...])


def _ple(h, p, wg, wp, g, *, tm=256):
    n, d = h.shape
    pd = p.shape[1]
    return pl.pallas_call(
        _ple_kernel,
        grid=(n // tm,),
        in_specs=[
            pl.BlockSpec((tm, d), lambda i: (i, 0)),
            pl.BlockSpec((tm, pd), lambda i: (i, 0)),
            pl.BlockSpec((d, d), lambda i: (0, 0)),
            pl.BlockSpec((pd, d), lambda i: (0, 0)),
            pl.BlockSpec((1, d), lambda i: (0, 0)),
        ],
        out_specs=pl.BlockSpec((tm, d), lambda i: (i, 0)),
        out_shape=jax.ShapeDtypeStruct((n, d), F32),
        compiler_params=_params("parallel"),
        name="ple",
    )(h, p, wg, wp, g)


def _swap_halves(w):
    half = w.shape[-1] // 2
    return jnp.concatenate([-w[..., half:], w[..., :half]], axis=-1)


def _rope_tables(pos):
    inv = 1.0 / (ROPE_BASE ** (jnp.arange(0, ROPE_DIM, 2, dtype=F32) / ROPE_DIM))
    ang = pos.astype(F32)[:, None] * inv[None, :]
    zeros = jnp.zeros((pos.shape[0], 128 - ROPE_DIM), F32)
    cos, sin = jnp.cos(ang), jnp.sin(ang)
    return (jnp.concatenate([cos, cos, zeros], axis=1), jnp.concatenate([sin, sin, zeros], axis=1))


def kernel(x_prompt, x_sample, cache_mla, cache_diff_k, cache_diff_v, page_table, p_prompt, p_sample,
           ffn1_pre_g, ffn1_wg, ffn1_wu, ffn1_wd, ffn1_post_g, mix_pre_g, w_in, q_a_norm_g, w_uq,
           kv_a_norm_g, w_uk, w_uv, lambda_q1, lambda_k1, lambda_q2, lambda_k2, diff_subln_g, w_o,
           mix_post_g, ffn2_pre_g, ffn2_wg, ffn2_wu, ffn2_wd, ffn2_post_g, w_ple, w_ple_gate, ple_post_g):
    assert ffn1_wg.shape[0] == 1, "single-layer stack expected"
    batch, seq, d = x_prompt.shape
    n_seq, t_new, _ = x_sample.shape
    n_p, n_s = batch * seq, n_seq * t_new
    past = page_table.shape[1] * cache_mla.shape[2]
    page = cache_mla.shape[2]

    x = jnp.concatenate([x_prompt.reshape(n_p, d), x_sample.reshape(n_s, d)], axis=0)
    pe = jnp.concatenate([p_prompt[0].reshape(n_p, -1), p_sample[0].reshape(n_s, -1)], axis=0)
    pos = jnp.concatenate([jnp.tile(jnp.arange(seq), batch), jnp.tile(past + jnp.arange(t_new), n_seq)])
    cos_t, sin_t = _rope_tables(pos)

    w = w_in[0]
    o1 = Q_LORA
    o2 = o1 + KV_LORA
    o3 = o2 + ROPE_DIM
    o4 = o3 + DIFF_HEADS * 2 * DIFF_HEAD_DIM
    o5 = o4 + 2 * DIFF_HEAD_DIM
    o6 = o5 + 2 * DIFF_HEAD_DIM
    w_rope = w[:, o2:o3]
    w_all = jnp.concatenate([
        w[:, o6:], w[:, o3:o4], w[:, :o1], w[:, o1:o2], w[:, o4:o5], w[:, o5:o6],
        w_rope, _swap_halves(w_rope), jnp.zeros((d, PROJ_COLS - COL_ROPE - 128), F32)], axis=1).astype(BF16)

    uq = w_uq[0].reshape(Q_LORA, MLA_HEADS, NOPE_DIM + ROPE_DIM)
    uq_rope = uq[:, :, NOPE_DIM:]
    wq = jnp.concatenate([uq[:, :, :NOPE_DIM], uq_rope, _swap_halves(uq_rope)], axis=-1)
    wq = wq.reshape(Q_LORA, MLA_HEADS * 256).astype(BF16)
    wuk = w_uk[0].reshape(KV_LORA, MLA_HEADS * NOPE_DIM).astype(BF16)
    wukt = jnp.transpose(w_uk[0], (1, 2, 0)).astype(BF16)
    wuv = w_uv[0].reshape(KV_LORA, MLA_HEADS * V_DIM).astype(BF16)
    wuvh = jnp.transpose(w_uv[0], (1, 0, 2)).astype(BF16)
    lam_vecs = (lambda_q1, lambda_k1, lambda_q2, lambda_k2)
    bf = lambda a: a[0].astype(BF16)

    h1 = _ffn(x, ffn1_pre_g, bf(ffn1_wg), bf(ffn1_wu), bf(ffn1_wd), ffn1_post_g)
    proj = _inproj(h1, mix_pre_g, w_all)

    kv_mla, k_up, v_up = _kv_side(proj, cos_t, sin_t, kv_a_norm_g, wuk, wuv)
    q = _q_side(proj, cos_t, sin_t, q_a_norm_g, wq)

    o_mla_p = _prompt_mla(q, k_up, v_up, batch, seq)
    o_diff_p = _prompt_diff(proj, lam_vecs, diff_subln_g, batch, seq)

    rows = t_new * MLA_HEADS
    qlat = _q_absorb(q, wukt, n_p, n_s).reshape(n_seq, rows, KV_LORA)
    qrot = q[n_p:].reshape(n_s, MLA_HEADS, 256)[:, :, NOPE_DIM:NOPE_DIM + ROPE_DIM].reshape(n_seq, rows, ROPE_DIM)
    pad_new = lambda a: jnp.pad(a.reshape(n_seq, t_new, -1), ((0, 0), (0, page - t_new), (0, 0)))
    lat_s = _decode_mla(page_table, qlat, qrot, pad_new(kv_mla[n_p:]), cache_mla)
    o_mla_s = _v_up(lat_s.reshape(n_s, MLA_HEADS * KV_LORA), wuvh)

    dq = proj[n_p:, COL_DQ:COL_DQ + DIFF_HEADS * 2 * DIFF_HEAD_DIM]
    dq = dq.reshape(n_seq, t_new * DIFF_HEADS, 2, DIFF_HEAD_DIM)
    zq = jnp.zeros_like(dq[:, :, 0])
    q_bd = jnp.concatenate([jnp.concatenate([dq[:, :, 0], zq], axis=-1),
                            jnp.concatenate([zq, dq[:, :, 1]], axis=-1)], axis=1).astype(BF16)
    dk_new = proj[:, COL_DK:COL_DK + 2 * DIFF_HEAD_DIM]
    dv_new = proj[:, COL_DV:COL_DV + 2 * DIFF_HEAD_DIM]
    o_diff_s = _decode_diff(page_table, q_bd, pad_new(dk_new[n_p:]), pad_new(dv_new[n_p:]), lam_vecs,
                            diff_subln_g, cache_diff_k, cache_diff_v)

    o_mla = jnp.concatenate([o_mla_p, o_mla_s], axis=0)
    o_diff = jnp.concatenate([o_diff_p, o_diff_s.reshape(n_s, d)], axis=0)
    h2 = _merge(proj, o_mla, o_diff, h1, bf(w_o), mix_post_g)
    h3 = _ffn(h2, ffn2_pre_g, bf(ffn2_wg), bf(ffn2_wu), bf(ffn2_wd), ffn2_post_g)
    y = _ple(h3, pe, bf(w_ple_gate), bf(w_ple), ple_post_g)

    def split(a, width):
        return (a[:n_p].reshape(1, batch, seq, width), a[n_p:].reshape(1, n_seq, t_new, width))

    mla_p, mla_s = split(kv_mla, KV_LORA + ROPE_DIM)
    dk_p, dk_s = split(dk_new, 2 * DIFF_HEAD_DIM)
    dv_p, dv_s = split(dv_new, 2 * DIFF_HEAD_DIM)
    return (y[:n_p].reshape(batch, seq, d), y[n_p:].reshape(n_seq, t_new, d),
            mla_p, dk_p, dv_p, mla_s, dk_s, dv_s)
```

```python
import functools
import math

import jax
import jax.numpy as jnp
from jax import lax
from jax.experimental import pallas as pl
from jax.experimental.pallas import tpu as pltpu

F32 = jnp.float32
BF16 = jnp.bfloat16

NORM_EPS = 1e-6
MLA_HEADS = 16
NOPE_DIM = 128
ROPE_DIM = 64
KV_LORA = 512
Q_LORA = 512
V_DIM = 128
MLA_SCALE = (NOPE_DIM + ROPE_DIM) ** -0.5
ROPE_BASE = 10000.0
DIFF_HEADS = 8
DIFF_HEAD_DIM = 128
DIFF_SCALE = DIFF_HEAD_DIM ** -0.5
LAM_INIT = 0.8 - 0.6 * math.exp(-0.3 * 0)
LOG2E = math.log2(math.e)

COL_GATE = 0
COL_DQ = 4096
COL_CQ = 6144
COL_LAT = 6656
COL_DK = 7168
COL_DV = 7424
COL_ROPE = 7680
PROJ_COLS = 8192

VMEM_LIMIT = 52 * 1024 * 1024


def _params(*sem):
    return pltpu.CompilerParams(dimension_semantics=sem, vmem_limit_bytes=VMEM_LIMIT)


def _tile(n, t):
    return math.gcd(n, t)


def _rms(x, g):
    return x * lax.rsqrt(jnp.mean(x * x, axis=-1, keepdims=True) + NORM_EPS) * g


def _dot(a, b):
    return jnp.dot(a, b, preferred_element_type=F32)


def _dot_nt(a, b):
    return lax.dot_general(a, b, (((1,), (1,)), ((), ())), preferred_element_type=F32)


def _ffn_kernel(x_ref, gpre_ref, wg_ref, wu_ref, wd_ref, gpost_ref, o_ref, xn_ref, acc_ref):
    j = pl.program_id(1)

    @pl.when(j == 0)
    def _():
        xn_ref[...] = _rms(x_ref[...], gpre_ref[...]).astype(BF16)
        acc_ref[...] = jnp.zeros_like(acc_ref)

    xn = xn_ref[...]
    g = _dot(xn, wg_ref[...])
    u = _dot(xn, wu_ref[...])
    mid = (g * jax.nn.sigmoid(g) * u).astype(BF16)
    acc_ref[...] += _dot(mid, wd_ref[...])

    @pl.when(j == pl.num_programs(1) - 1)
    def _():
        o_ref[...] = x_ref[...] + 0.5 * _rms(acc_ref[...], gpost_ref[...])


def _ffn(x, gpre, wg, wu, wd, gpost, *, tm=512, tf=512):
    n, d = x.shape
    dff = wg.shape[1]
    tm = _tile(n, tm)
    return pl.pallas_call(
        _ffn_kernel,
        grid=(n // tm, dff // tf),
        in_specs=[
            pl.BlockSpec((tm, d), lambda i, j: (i, 0)),
            pl.BlockSpec((1, d), lambda i, j: (0, 0)),
            pl.BlockSpec((d, tf), lambda i, j: (0, j)),
            pl.BlockSpec((d, tf), lambda i, j: (0, j)),
            pl.BlockSpec((tf, d), lambda i, j: (j, 0)),
            pl.BlockSpec((1, d), lambda i, j: (0, 0)),
        ],
        out_specs=pl.BlockSpec((tm, d), lambda i, j: (i, 0)),
        out_shape=jax.ShapeDtypeStruct((n, d), F32),
        scratch_shapes=[pltpu.VMEM((tm, d), BF16), pltpu.VMEM((tm, d), F32)],
        compiler_params=_params("parallel", "arbitrary"),
        name="ffn",
    )(x, gpre, wg, wu, wd, gpost)


def _inproj_kernel(h_ref, g_ref, w_ref, o_ref, u_ref):
    @pl.when(pl.program_id(1) == 0)
    def _():
        u_ref[...] = _rms(h_ref[...], g_ref[...]).astype(BF16)

    o_ref[...] = _dot(u_ref[...], w_ref[...])


def _inproj(h, g, w, *, tm=1024, tn=512):
    n, d = h.shape
    cols = w.shape[1]
    tm = _tile(n, tm)
    return pl.pallas_call(
        _inproj_kernel,
        grid=(n // tm, cols // tn),
        in_specs=[
            pl.BlockSpec((tm, d), lambda i, j: (i, 0)),
            pl.BlockSpec((1, d), lambda i, j: (0, 0)),
            pl.BlockSpec((d, tn), lambda i, j: (0, j)),
        ],
        out_specs=pl.BlockSpec((tm, tn), lambda i, j: (i, j)),
        out_shape=jax.ShapeDtypeStruct((n, cols), F32),
        scratch_shapes=[pltpu.VMEM((tm, d), BF16)],
        compiler_params=_params("parallel", "arbitrary"),
        name="inproj",
    )(h, g, w)


def _rope_tile(t, c, s):
    return t * c + pltpu.roll(t, 64, 1) * s


def _kv_kernel(lat_ref, rope_ref, c_ref, s_ref, g_ref, wuk_ref, wuv_ref, kv_ref, k_ref, v_ref):
    latn = _rms(lat_ref[...], g_ref[...])
    rot = _rope_tile(rope_ref[...], c_ref[...], s_ref[...])
    kv_ref[:, :KV_LORA] = latn
    kv_ref[:, KV_LORA:] = rot[:, :ROPE_DIM]
    lb = latn.astype(BF16)
    kn = _dot(lb, wuk_ref[...])
    v_ref[...] = _dot(lb, wuv_ref[...]).astype(BF16)
    rotb = rot.astype(BF16)
    for h in range(MLA_HEADS):
        k_ref[:, h * 256:h * 256 + 128] = kn[:, h * 128:(h + 1) * 128].astype(BF16)
        k_ref[:, h * 256 + 128:(h + 1) * 256] = rotb


def _kv_side(proj, cos_t, sin_t, g, wuk, wuv, *, tm=256):
    n = proj.shape[0]
    hd = MLA_HEADS * V_DIM
    return pl.pallas_call(
        _kv_kernel,
        grid=(n // tm,),
        in_specs=[
            pl.BlockSpec((tm, KV_LORA), lambda i: (i, COL_LAT // KV_LORA)),
            pl.BlockSpec((tm, 128), lambda i: (i, COL_ROPE // 128)),
            pl.BlockSpec((tm, 128), lambda i: (i, 0)),
            pl.BlockSpec((tm, 128), lambda i: (i, 0)),
            pl.BlockSpec((1, KV_LORA), lambda i: (0, 0)),
            pl.BlockSpec((KV_LORA, hd), lambda i: (0, 0)),
            pl.BlockSpec((KV_LORA, hd), lambda i: (0, 0)),
        ],
        out_specs=[
            pl.BlockSpec((tm, KV_LORA + ROPE_DIM), lambda i: (i, 0)),
            pl.BlockSpec((tm, 2 * hd), lambda i: (i, 0)),
            pl.BlockSpec((tm, hd), lambda i: (i, 0)),
        ],
        out_shape=[
            jax.ShapeDtypeStruct((n, KV_LORA + ROPE_DIM), F32),
            jax.ShapeDtypeStruct((n, 2 * hd), BF16),
            jax.ShapeDtypeStruct((n, hd), BF16),
        ],
        compiler_params=_params("parallel"),
        name="kv_side",
    )(proj, proj, cos_t, sin_t, g, wuk, wuv)


def _q_kernel(cq_ref, c_ref, s_ref, g_ref, wq_ref, q_ref):
    cqn = _rms(cq_ref[...], g_ref[...]).astype(BF16)
    qa = _dot(cqn, wq_ref[...]) * (MLA_SCALE * LOG2E)
    c = c_ref[...]
    s = s_ref[...]
    for h in range(MLA_HEADS):
        q_ref[:, h * 256:h * 256 + 128] = qa[:, h * 256:h * 256 + 128].astype(BF16)
        t = qa[:, h * 256 + 128:(h + 1) * 256]
        q_ref[:, h * 256 + 128:(h + 1) * 256] = _rope_tile(t, c, s).astype(BF16)


def _q_side(proj, cos_t, sin_t, g, wq, *, tm=256):
    n = proj.shape[0]
    cols = wq.shape[1]
    return pl.pallas_call(
        _q_kernel,
        grid=(n // tm,),
        in_specs=[
            pl.BlockSpec((tm, Q_LORA), lambda i: (i, COL_CQ // Q_LORA)),
            pl.BlockSpec((tm, 128), lambda i: (i, 0)),
            pl.BlockSpec((tm, 128), lambda i: (i, 0)),
            pl.BlockSpec((1, Q_LORA), lambda i: (0, 0)),
            pl.BlockSpec((Q_LORA, cols), lambda i: (0, 0)),
        ],
        out_specs=pl.BlockSpec((tm, cols), lambda i: (i, 0)),
        out_shape=jax.ShapeDtypeStruct((n, cols), BF16),
        compiler_params=_params("parallel"),
        name="q_side",
    )(proj, cos_t, sin_t, g, wq)


def _qlat_kernel(q_ref, w_ref, o_ref):
    o_ref[...] = _dot(q_ref[:, :NOPE_DIM], w_ref[...]).astype(BF16)


def _q_absorb(q, wukt, row0, n_s, *, tm=256):
    return pl.pallas_call(
        _qlat_kernel,
        grid=(n_s // tm, MLA_HEADS),
        in_specs=[
            pl.BlockSpec((tm, 256), lambda i, h: (row0 // tm + i, h)),
            pl.BlockSpec((None, NOPE_DIM, KV_LORA), lambda i, h: (h, 0, 0)),
        ],
        out_specs=pl.BlockSpec((tm, KV_LORA), lambda i, h: (i, h)),
        out_shape=jax.ShapeDtypeStruct((n_s, MLA_HEADS * KV_LORA), BF16),
        compiler_params=_params("parallel", "arbitrary"),
        name="q_absorb",
    )(q, wukt)


def _uv_kernel(lat_ref, w_ref, o_ref):
    o_ref[...] = _dot(lat_ref[...].astype(BF16), w_ref[...])


def _v_up(lat, wuvh, *, tm=256):
    n_s = lat.shape[0]
    return pl.pallas_call(
        _uv_kernel,
        grid=(n_s // tm, MLA_HEADS),
        in_specs=[
            pl.BlockSpec((tm, KV_LORA), lambda i, h: (i, h)),
            pl.BlockSpec((None, KV_LORA, V_DIM), lambda i, h: (h, 0, 0)),
        ],
        out_specs=pl.BlockSpec((tm, V_DIM), lambda i, h: (i, h)),
        out_shape=jax.ShapeDtypeStruct((n_s, MLA_HEADS * V_DIM), F32),
        compiler_params=_params("parallel", "arbitrary"),
        name="v_up",
    )(lat, wuvh)


def _softmax_step(s, v, m, l, acc):
    m_new = jnp.maximum(m, jnp.max(s, axis=-1, keepdims=True))
    alpha = jnp.exp2(m - m_new)
    p = jnp.exp2(s - m_new)
    l_new = alpha * l + jnp.sum(p, axis=-1, keepdims=True)
    acc_new = alpha * acc + _dot(p.astype(BF16), v)
    return m_new, l_new, acc_new


def _softmax_part(s):
    m = jnp.max(s, axis=-1, keepdims=True)
    p = jnp.exp2(s - m)
    return m, jnp.sum(p, axis=-1, keepdims=True), p.astype(BF16)


def _merge_parts(m_ref, l_ref, acc_ref, parts):
    m_old = m_ref[...]
    m_new = m_old
    for m, _, _ in parts:
        m_new = jnp.maximum(m_new, m)
    alpha = jnp.exp2(m_old - m_new)
    l = alpha * l_ref[...]
    acc = alpha * acc_ref[...]
    for m, lp, ap in parts:
        w = jnp.exp2(m - m_new)
        l = l + w * lp
        acc = acc + w * ap
    m_ref[...] = m_new
    l_ref[...] = l
    acc_ref[...] = acc


def _init_state(m_ref, l_ref, acc_ref):
    m_ref[...] = jnp.full_like(m_ref, -jnp.inf)
    l_ref[...] = jnp.zeros_like(l_ref)
    acc_ref[...] = jnp.zeros_like(acc_ref)


def _causal(n):
    row = lax.broadcasted_iota(jnp.int32, (n, n), 0)
    col = lax.broadcasted_iota(jnp.int32, (n, n), 1)
    return col <= row


def _pmla_kernel(q_ref, k_ref, v_ref, o_ref, *, tq):
    qi = pl.program_id(2)
    q = q_ref[...]
    tk = tq // 2

    def span(off, state, masks=None):
        o0 = pl.multiple_of(off, tk)
        o1 = pl.multiple_of(off + tk, tk)
        s0 = _dot_nt(q, k_ref[pl.ds(o0, tk), :])
        s1 = _dot_nt(q, k_ref[pl.ds(o1, tk), :])
        if masks is not None:
            s0 = jnp.where(masks[0], s0, -jnp.inf)
            s1 = jnp.where(masks[1], s1, -jnp.inf)
        state = _softmax_step(s0, v_ref[pl.ds(o0, tk), :], *state)
        return _softmax_step(s1, v_ref[pl.ds(o1, tk), :], *state)

    init = (jnp.full((tq, 1), -jnp.inf, F32), jnp.zeros((tq, 1), F32), jnp.zeros((tq, V_DIM), F32))
    state = lax.fori_loop(0, qi, lambda i, st: span(i * tq, st), init)
    row = lax.broadcasted_iota(jnp.int32, (tq, tk), 0)
    col = lax.broadcasted_iota(jnp.int32, (tq, tk), 1)
    m, l, acc = span(qi * tq, state, (col <= row, col + tk <= row))
    o_ref[...] = acc / l


def _prompt_mla(q, k, v, batch, seq, *, tq=1024):
    tq = _tile(seq, tq)
    nq = seq // tq
    return pl.pallas_call(
        functools.partial(_pmla_kernel, tq=tq),
        grid=(batch, MLA_HEADS, nq),
        in_specs=[
            pl.BlockSpec((tq, 256), lambda b, h, i: (b * nq + i, h)),
            pl.BlockSpec((seq, 256), lambda b, h, i: (b, h)),
            pl.BlockSpec((seq, V_DIM), lambda b, h, i: (b, h)),
        ],
        out_specs=pl.BlockSpec((tq, V_DIM), lambda b, h, i: (b * nq + i, h)),
        out_shape=jax.ShapeDtypeStruct((batch * seq, MLA_HEADS * V_DIM), F32),
        compiler_params=_params("parallel", "parallel", "arbitrary"),
        name="prompt_mla",
    )(q, k, v)


def _lambda(lq1_ref, lk1_ref, lq2_ref, lk2_ref):
    a = jnp.sum(lq1_ref[...] * lk1_ref[...], axis=-1, keepdims=True)
    b = jnp.sum(lq2_ref[...] * lk2_ref[...], axis=-1, keepdims=True)
    return jnp.exp(a) - jnp.exp(b) + LAM_INIT


def _diff_finish(a1, l1, a2, l2, lam, g):
    out = a1 / l1 - lam * (a2 / l2)
    return _rms(out, g) * (1.0 - LAM_INIT)


def _slope2(head):
    return jnp.exp2(-(head + 1).astype(F32) * (8.0 / DIFF_HEADS)) * LOG2E


def _pdiff_kernel(q_ref, k_ref, v_ref, lq1_ref, lk1_ref, lq2_ref, lk2_ref, g_ref, o_ref, kb_ref, vb_ref,
                  *, tq, sub):
    h = pl.program_id(1)
    qi = pl.program_id(2)
    nsub = tq // sub
    hd = DIFF_HEAD_DIM

    @pl.when((h == 0) & (qi == 0))
    def _():
        kb_ref[...] = k_ref[...].astype(BF16)
        vb_ref[...] = v_ref[...].astype(BF16)

    slope = _slope2(jnp.full((1, 1), h, jnp.int32))
    qs = []
    for a in range(nsub):
        q = q_ref[a * sub:(a + 1) * sub, :] * (DIFF_SCALE * LOG2E)
        qs += [q[:, :hd].astype(BF16), q[:, hd:].astype(BF16)]

    def bias_row(start, n):
        rel = lax.broadcasted_iota(jnp.int32, (1, n), 1) + (start - qi * tq)
        return slope * rel.astype(F32)

    def attend(c, k, v, bias, state, mask=None):
        s = _dot_nt(qs[c], k[:, (c % 2) * hd:(c % 2 + 1) * hd]) + bias
        if mask is not None:
            s = jnp.where(mask, s, -jnp.inf)
        return _softmax_step(s, v, *state)

    def full_block(kb, carry):
        off = pl.multiple_of(kb * tq, tq)
        k = kb_ref[pl.ds(off, tq), :]
        v = vb_ref[pl.ds(off, tq), :]
        bias = bias_row(off, tq)
        return tuple(attend(c, k, v, bias, carry[c]) for c in range(2 * nsub))

    init = tuple((jnp.full((sub, 1), -jnp.inf, F32), jnp.zeros((sub, 1), F32), jnp.zeros((sub, 2 * hd), F32))
                 for _ in range(2 * nsub))
    carry = lax.fori_loop(0, qi, full_block, init)

    off = pl.multiple_of(qi * tq, tq)
    mask = _causal(sub)
    lam = _lambda(lq1_ref, lk1_ref, lq2_ref, lk2_ref)
    for a in range(nsub):
        states = [carry[2 * a], carry[2 * a + 1]]
        if a > 0:
            k = kb_ref[pl.ds(off, a * sub), :]
            v = vb_ref[pl.ds(off, a * sub), :]
            bias = bias_row(off, a * sub)
            states = [attend(2 * a + j, k, v, bias, states[j]) for j in range(2)]
        offd = pl.multiple_of(off + a * sub, sub)
        k = kb_ref[pl.ds(offd, sub), :]
        v = vb_ref[pl.ds(offd, sub), :]
        bias = bias_row(offd, sub)
        (_, l1, a1), (_, l2, a2) = [attend(2 * a + j, k, v, bias, states[j], mask) for j in range(2)]
        o_ref[a * sub:(a + 1) * sub, :] = _diff_finish(a1, l1, a2, l2, lam, g_ref[...])


def _prompt_diff(proj, lam_vecs, g, batch, seq, *, tq=512, sub=256):
    nq = seq // tq
    w = 2 * DIFF_HEAD_DIM
    vec = pl.BlockSpec((1, DIFF_HEAD_DIM), lambda b, h, i: (0, 0))
    return pl.pallas_call(
        functools.partial(_pdiff_kernel, tq=tq, sub=sub),
        grid=(batch, DIFF_HEADS, nq),
        in_specs=[
            pl.BlockSpec((tq, w), lambda b, h, i: (b * nq + i, COL_DQ // w + h)),
            pl.BlockSpec((seq, w), lambda b, h, i: (b, COL_DK // w)),
            pl.BlockSpec((seq, w), lambda b, h, i: (b, COL_DV // w)),
            vec, vec, vec, vec,
            pl.BlockSpec((1, w), lambda b, h, i: (0, 0)),
        ],
        out_specs=pl.BlockSpec((tq, w), lambda b, h, i: (b * nq + i, h)),
        out_shape=jax.ShapeDtypeStruct((batch * seq, DIFF_HEADS * w), F32),
        scratch_shapes=[pltpu.VMEM((seq, w), BF16), pltpu.VMEM((seq, w), BF16)],
        compiler_params=_params("arbitrary", "arbitrary", "arbitrary"),
        name="prompt_diff",
    )(proj, proj, proj, *lam_vecs, g)


def _dmla_kernel(pt_ref, qlat_ref, qrot_ref, new_ref, *rest, pages, group):
    page_refs = rest[:pages]
    o_ref = rest[pages]
    m_ref, l_ref, acc_ref = rest[pages + 1:]
    p = pl.program_id(1)
    rows = qlat_ref.shape[0]

    @pl.when(p == 0)
    def _():
        _init_state(m_ref, l_ref, acc_ref)

    qlat = qlat_ref[...]
    qrot = qrot_ref[...]

    def attend(kt, mask=None):
        lat = kt[:KV_LORA]
        s = _dot(qlat, lat) + _dot(qrot, kt[KV_LORA:])
        if mask is not None:
            s = jnp.where(mask, s, -jnp.inf)
        m, l, pb = _softmax_part(s)
        return m, l, _dot_nt(pb, lat)

    parts = []
    for g0 in range(0, pages, group):
        kt = jnp.concatenate([page_refs[g0 + j][...].astype(BF16) for j in range(group)], axis=1)
        parts.append(attend(kt))
    _merge_parts(m_ref, l_ref, acc_ref, parts)

    @pl.when(p == pl.num_programs(1) - 1)
    def _():
        n_new = new_ref.shape[1]
        t = lax.broadcasted_iota(jnp.int32, (rows, n_new), 0) // MLA_HEADS
        c = lax.broadcasted_iota(jnp.int32, (rows, n_new), 1)
        _merge_parts(m_ref, l_ref, acc_ref, [attend(new_ref[...].astype(BF16), c <= t)])
        o_ref[...] = acc_ref[...] / l_ref[...]


def _decode_mla(page_table, qlat, qrot, kv_new_t, cache_t, *, pages=16, group=8):
    n_seq, rows, _ = qlat.shape
    n_pages = page_table.shape[1]
    width, page = cache_t.shape[2], cache_t.shape[3]
    pages = _tile(n_pages, pages)
    group = _tile(pages, group)
    steps = n_pages // pages

    def page_spec(kk):
        return pl.BlockSpec((None, None, width, page),
                            lambda n, p, pt: (0, pt[n * n_pages + p * pages + kk], 0, 0))

    grid_spec = pltpu.PrefetchScalarGridSpec(
        num_scalar_prefetch=1,
        grid=(n_seq, steps),
        in_specs=[
            pl.BlockSpec((None, rows, KV_LORA), lambda n, p, pt: (n, 0, 0)),
            pl.BlockSpec((None, rows, ROPE_DIM), lambda n, p, pt: (n, 0, 0)),
            pl.BlockSpec((None, width, page), lambda n, p, pt: (n, 0, 0)),
        ] + [page_spec(kk) for kk in range(pages)],
        out_specs=pl.BlockSpec((None, rows, KV_LORA), lambda n, p, pt: (n, 0, 0)),
        scratch_shapes=[
            pltpu.VMEM((rows, 1), F32),
            pltpu.VMEM((rows, 1), F32),
            pltpu.VMEM((rows, KV_LORA), F32),
        ],
    )
    return pl.pallas_call(
        functools.partial(_dmla_kernel, pages=pages, group=group),
        grid_spec=grid_spec,
        out_shape=jax.ShapeDtypeStruct((n_seq, rows, KV_LORA), F32),
        compiler_params=_params("parallel", "arbitrary"),
        name="decode_mla",
    )(page_table.reshape(-1), qlat, qrot, kv_new_t, *([cache_t] * pages))


def _ddiff_kernel(pt_ref, q_ref, knew_ref, vnew_ref, lq1_ref, lk1_ref, lq2_ref, lk2_ref, g_ref, *rest,
                  pages, group, page, past):
    k_refs = rest[:pages]
    v_refs = rest[pages:2 * pages]
    o_ref = rest[2 * pages]
    m_ref, l_ref, acc_ref = rest[2 * pages + 1:]
    p = pl.program_id(1)
    rows = q_ref.shape[0]
    half = rows // 2

    @pl.when(p == 0)
    def _():
        _init_state(m_ref, l_ref, acc_ref)

    q = (q_ref[...] * (DIFF_SCALE * LOG2E)).astype(BF16)
    r = lax.broadcasted_iota(jnp.int32, (rows, 1), 0)
    slope = _slope2(r % DIFF_HEADS)

    def attend(k, v, kpos0, mask=None):
        n_keys = k.shape[0]
        rel = lax.broadcasted_iota(jnp.int32, (1, n_keys), 1) + (kpos0 - past)
        s = _dot_nt(q, k) + slope * rel.astype(F32)
        if mask is not None:
            s = jnp.where(mask, s, -jnp.inf)
        m, l, pb = _softmax_part(s)
        return m, l, _dot(pb, v)

    parts = []
    for g0 in range(0, pages, group):
        k = jnp.concatenate([k_refs[g0 + j][...].astype(BF16) for j in range(group)], axis=0)
        v = jnp.concatenate([v_refs[g0 + j][...].astype(BF16) for j in range(group)], axis=0)
        parts.append(attend(k, v, (p * pages + g0) * page))
    _merge_parts(m_ref, l_ref, acc_ref, parts)

    @pl.when(p == pl.num_programs(1) - 1)
    def _():
        n_new = knew_ref.shape[0]
        t = (lax.broadcasted_iota(jnp.int32, (rows, n_new), 0) % half) // DIFF_HEADS
        c = lax.broadcasted_iota(jnp.int32, (rows, n_new), 1)
        new = attend(knew_ref[...].astype(BF16), vnew_ref[...].astype(BF16), past, c <= t)
        _merge_parts(m_ref, l_ref, acc_ref, [new])
        acc = acc_ref[...]
        l = l_ref[...]
        lam = _lambda(lq1_ref, lk1_ref, lq2_ref, lk2_ref)
        o_ref[...] = _diff_finish(acc[:half], l[:half], acc[half:], l[half:], lam, g_ref[...])


def _decode_diff(page_table, q_bd, k_new, v_new, lam_vecs, g, cache_k, cache_v, *, pages=16, group=8):
    n_seq, rows, w = q_bd.shape
    n_pages = page_table.shape[1]
    page = cache_k.shape[2]
    pages = _tile(n_pages, pages)
    group = _tile(pages, group)
    steps = n_pages // pages
    past = n_pages * page

    def page_spec(kk):
        return pl.BlockSpec((None, None, page, w),
                            lambda n, p, pt: (0, pt[n * n_pages + p * pages + kk], 0, 0))

    vec = pl.BlockSpec((1, DIFF_HEAD_DIM), lambda n, p, pt: (0, 0))
    grid_spec = pltpu.PrefetchScalarGridSpec(
        num_scalar_prefetch=1,
        grid=(n_seq, steps),
        in_specs=[
            pl.BlockSpec((None, rows, w), lambda n, p, pt: (n, 0, 0)),
            pl.BlockSpec((None, page, w), lambda n, p, pt: (n, 0, 0)),
            pl.BlockSpec((None, page, w), lambda n, p, pt: (n, 0, 0)),
            vec, vec, vec, vec,
            pl.BlockSpec((1, w), lambda n, p, pt: (0, 0)),
        ] + [page_spec(kk) for kk in range(pages)] * 2,
        out_specs=pl.BlockSpec((None, rows // 2, w), lambda n, p, pt: (n, 0, 0)),
        scratch_shapes=[
            pltpu.VMEM((rows, 1), F32),
            pltpu.VMEM((rows, 1), F32),
            pltpu.VMEM((rows, w), F32),
        ],
    )
    return pl.pallas_call(
        functools.partial(_ddiff_kernel, pages=pages, group=group, page=page, past=past),
        grid_spec=grid_spec,
        out_shape=jax.ShapeDtypeStruct((n_seq, rows // 2, w), F32),
        compiler_params=_params("parallel", "arbitrary"),
        name="decode_diff",
    )(page_table.reshape(-1), q_bd, k_new, v_new, *lam_vecs, g,
      *([cache_k] * pages), *([cache_v] * pages))


def _merge_kernel(g0_ref, g1_ref, om_ref, od_ref, h_ref, wo_ref, g_ref, o_ref):
    merged = jax.nn.sigmoid(g0_ref[...]) * om_ref[...] + jax.nn.sigmoid(g1_ref[...]) * od_ref[...]
    y = _dot(merged.astype(BF16), wo_ref[...])
    o_ref[...] = h_ref[...] + _rms(y, g_ref[...])


def _merge(proj, o_mla, o_diff, h, wo, g, *, tm=256):
    n, d = h.shape
    row = lambda c: pl.BlockSpec((tm, d), lambda i: (i, c))
    return pl.pallas_call(
        _merge_kernel,
        grid=(n // tm,),
        in_specs=[
            row(COL_GATE // d), row(COL_GATE // d + 1), row(0), row(0), row(0),
            pl.BlockSpec((d, d), lambda i: (0, 0)),
            pl.BlockSpec((1, d), lambda i: (0, 0)),
        ],
        out_specs=row(0),
        out_shape=jax.ShapeDtypeStruct((n, d), F32),
        compiler_params=_params("parallel"),
        name="merge",
    )(proj, proj, o_mla, o_diff, h, wo, g)


def _ple_kernel(h_ref, p_ref, wg_ref, wp_ref, g_ref, o_ref):
    h = h_ref[...]
    gate = jax.nn.sigmoid(_dot(h.astype(BF16), wg_ref[...]))
    emb = _dot(p_ref[...].astype(BF16), wp_ref[...])
    o_ref[...] = h + _rms(gate * emb, g_ref[...])


def _ple(h, p, wg, wp, g, *, tm=256):
    n, d = h.shape
    pd = p.shape[1]
    return pl.pallas_call(
        _ple_kernel,
        grid=(n // tm,),
        in_specs=[
            pl.BlockSpec((tm, d), lambda i: (i, 0)),
            pl.BlockSpec((tm, pd), lambda i: (i, 0)),
            pl.BlockSpec((d, d), lambda i: (0, 0)),
            pl.BlockSpec((pd, d), lambda i: (0, 0)),
            pl.BlockSpec((1, d), lambda i: (0, 0)),
        ],
        out_specs=pl.BlockSpec((tm, d), lambda i: (i, 0)),
        out_shape=jax.ShapeDtypeStruct((n, d), F32),
        compiler_params=_params("parallel"),
        name="ple",
    )(h, p, wg, wp, g)


def _swap_halves(w):
    half = w.shape[-1] // 2
    return jnp.concatenate([-w[..., half:], w[..., :half]], axis=-1)


def _rope_tables(pos):
    inv = 1.0 / (ROPE_BASE ** (jnp.arange(0, ROPE_DIM, 2, dtype=F32) / ROPE_DIM))
    ang = pos.astype(F32)[:, None] * inv[None, :]
    zeros = jnp.zeros((pos.shape[0], 128 - ROPE_DIM), F32)
    cos, sin = jnp.cos(ang), jnp.sin(ang)
    return (jnp.concatenate([cos, cos, zeros], axis=1), jnp.concatenate([sin, sin, zeros], axis=1))


def kernel(x_prompt, x_sample, cache_mla, cache_diff_k, cache_diff_v, page_table, p_prompt, p_sample,
           ffn1_pre_g, ffn1_wg, ffn1_wu, ffn1_wd, ffn1_post_g, mix_pre_g, w_in, q_a_norm_g, w_uq,
           kv_a_norm_g, w_uk, w_uv, lambda_q1, lambda_k1, lambda_q2, lambda_k2, diff_subln_g, w_o,
           mix_post_g, ffn2_pre_g, ffn2_wg, ffn2_wu, ffn2_wd, ffn2_post_g, w_ple, w_ple_gate, ple_post_g):
    assert ffn1_wg.shape[0] == 1, "single-layer stack expected"
    batch, seq, d = x_prompt.shape
    n_seq, t_new, _ = x_sample.shape
    n_p, n_s = batch * seq, n_seq * t_new
    past = page_table.shape[1] * cache_mla.shape[2]
    page = cache_mla.shape[2]

    x = jnp.concatenate([x_prompt.reshape(n_p, d), x_sample.reshape(n_s, d)], axis=0)
    pe = jnp.concatenate([p_prompt[0].reshape(n_p, -1), p_sample[0].reshape(n_s, -1)], axis=0)
    pos = jnp.concatenate([jnp.tile(jnp.arange(seq), batch), jnp.tile(past + jnp.arange(t_new), n_seq)])
    cos_t, sin_t = _rope_tables(pos)

    w = w_in[0]
    o1 = Q_LORA
    o2 = o1 + KV_LORA
    o3 = o2 + ROPE_DIM
    o4 = o3 + DIFF_HEADS * 2 * DIFF_HEAD_DIM
    o5 = o4 + 2 * DIFF_HEAD_DIM
    o6 = o5 + 2 * DIFF_HEAD_DIM
    w_rope = w[:, o2:o3]
    w_all = jnp.concatenate([
        w[:, o6:], w[:, o3:o4], w[:, :o1], w[:, o1:o2], w[:, o4:o5], w[:, o5:o6],
        w_rope, _swap_halves(w_rope), jnp.zeros((d, PROJ_COLS - COL_ROPE - 128), F32)], axis=1).astype(BF16)

    uq = w_uq[0].reshape(Q_LORA, MLA_HEADS, NOPE_DIM + ROPE_DIM)
    uq_rope = uq[:, :, NOPE_DIM:]
    wq = jnp.concatenate([uq[:, :, :NOPE_DIM], uq_rope, _swap_halves(uq_rope)], axis=-1)
    wq = wq.reshape(Q_LORA, MLA_HEADS * 256).astype(BF16)
    wuk = w_uk[0].reshape(KV_LORA, MLA_HEADS * NOPE_DIM).astype(BF16)
    wukt = jnp.transpose(w_uk[0], (1, 2, 0)).astype(BF16)
    wuv = w_uv[0].reshape(KV_LORA, MLA_HEADS * V_DIM).astype(BF16)
    wuvh = jnp.transpose(w_uv[0], (1, 0, 2)).astype(BF16)
    lam_vecs = (lambda_q1, lambda_k1, lambda_q2, lambda_k2)
    bf = lambda a: a[0].astype(BF16)

    h1 = _ffn(x, ffn1_pre_g, bf(ffn1_wg), bf(ffn1_wu), bf(ffn1_wd), ffn1_post_g)
    proj = _inproj(h1, mix_pre_g, w_all)

    kv_mla, k_up, v_up = _kv_side(proj, cos_t, sin_t, kv_a_norm_g, wuk, wuv)
    q = _q_side(proj, cos_t, sin_t, q_a_norm_g, wq)

    o_mla_p = _prompt_mla(q, k_up, v_up, batch, seq)
    o_diff_p = _prompt_diff(proj, lam_vecs, diff_subln_g, batch, seq)

    rows = t_new * MLA_HEADS
    qlat = _q_absorb(q, wukt, n_p, n_s).reshape(n_seq, rows, KV_LORA)
    qrot = q[n_p:].reshape(n_s, MLA_HEADS, 256)[:, :, NOPE_DIM:NOPE_DIM + ROPE_DIM].reshape(n_seq, rows, ROPE_DIM)
    pad_new = lambda a: jnp.pad(a.reshape(n_seq, t_new, -1), ((0, 0), (0, page - t_new), (0, 0)))
    cache_t = jnp.swapaxes(cache_mla, 2, 3)
    kv_new_t = jnp.swapaxes(pad_new(kv_mla[n_p:]), 1, 2)
    lat_s = _decode_mla(page_table, qlat, qrot, kv_new_t, cache_t)
    o_mla_s = _v_up(lat_s.reshape(n_s, MLA_HEADS * KV_LORA), wuvh)

    dq = proj[n_p:, COL_DQ:COL_DQ + DIFF_HEADS * 2 * DIFF_HEAD_DIM]
    dq = dq.reshape(n_seq, t_new * DIFF_HEADS, 2, DIFF_HEAD_DIM)
    zq = jnp.zeros_like(dq[:, :, 0])
    q_bd = jnp.concatenate([jnp.concatenate([dq[:, :, 0], zq], axis=-1),
                            jnp.concatenate([zq, dq[:, :, 1]], axis=-1)], axis=1)
    dk_new = proj[:, COL_DK:COL_DK + 2 * DIFF_HEAD_DIM]
    dv_new = proj[:, COL_DV:COL_DV + 2 * DIFF_HEAD_DIM]
    o_diff_s = _decode_diff(page_table, q_bd, pad_new(dk_new[n_p:]), pad_new(dv_new[n_p:]), lam_vecs,
                            diff_subln_g, cache_diff_k, cache_diff_v)

    o_mla = jnp.concatenate([o_mla_p, o_mla_s], axis=0)
    o_diff = jnp.concatenate([o_diff_p, o_diff_s.reshape(n_s, d)], axis=0)
    h2 = _merge(proj, o_mla, o_diff, h1, bf(w_o), mix_post_g)
    h3 = _ffn(h2, ffn2_pre_g, bf(ffn2_wg), bf(ffn2_wu), bf(ffn2_wd), ffn2_post_g)
    y = _ple(h3, pe, bf(w_ple_gate), bf(w_ple), ple_post_g)

    def split(a, width):
        return (a[:n_p].reshape(1, batch, seq, width), a[n_p:].reshape(1, n_seq, t_new, width))

    mla_p, mla_s = split(kv_mla, KV_LORA + ROPE_DIM)
    dk_p, dk_s = split(dk_new, 2 * DIFF_HEAD_DIM)
    dv_p, dv_s = split(dv_new, 2 * DIFF_HEAD_DIM)
    return (y[:n_p].reshape(batch, seq, d), y[n_p:].reshape(n_seq, t_new, d),
            mla_p, dk_p, dv_p, mla_s, dk_s, dv_s)
```

```python
import functools
import math

import jax
import jax.numpy as jnp
from jax import lax
from jax.experimental import pallas as pl
from jax.experimental.pallas import tpu as pltpu

F32 = jnp.float32
BF16 = jnp.bfloat16

NORM_EPS = 1e-6
MLA_HEADS = 16
NOPE_DIM = 128
ROPE_DIM = 64
KV_LORA = 512
Q_LORA = 512
V_DIM = 128
MLA_SCALE = (NOPE_DIM + ROPE_DIM) ** -0.5
ROPE_BASE = 10000.0
DIFF_HEADS = 8
DIFF_HEAD_DIM = 128
DIFF_SCALE = DIFF_HEAD_DIM ** -0.5
LAM_INIT = 0.8 - 0.6 * math.exp(-0.3 * 0)
LOG2E = math.log2(math.e)

COL_GATE = 0
COL_DQ = 4096
COL_CQ = 6144
COL_LAT = 6656
COL_DK = 7168
COL_DV = 7424
COL_ROPE = 7680
PROJ_COLS = 8192

VMEM_LIMIT = 52 * 1024 * 1024


def _params(*sem):
    return pltpu.CompilerParams(dimension_semantics=sem, vmem_limit_bytes=VMEM_LIMIT)


def _tile(n, t):
    return math.gcd(n, t)


def _rms(x, g):
    return x * lax.rsqrt(jnp.mean(x * x, axis=-1, keepdims=True) + NORM_EPS) * g


def _dot(a, b):
    return jnp.dot(a, b, preferred_element_type=F32)


def _dot_nt(a, b):
    return lax.dot_general(a, b, (((1,), (1,)), ((), ())), preferred_element_type=F32)


def _ffn_kernel(x_ref, gpre_ref, wg_ref, wu_ref, wd_ref, gpost_ref, o_ref, xn_ref, acc_ref):
    j = pl.program_id(1)

    @pl.when(j == 0)
    def _():
        xn_ref[...] = _rms(x_ref[...], gpre_ref[...]).astype(BF16)
        acc_ref[...] = jnp.zeros_like(acc_ref)

    xn = xn_ref[...]
    g = _dot(xn, wg_ref[...])
    u = _dot(xn, wu_ref[...])
    mid = (g * jax.nn.sigmoid(g) * u).astype(BF16)
    acc_ref[...] += _dot(mid, wd_ref[...])

    @pl.when(j == pl.num_programs(1) - 1)
    def _():
        o_ref[...] = x_ref[...] + 0.5 * _rms(acc_ref[...], gpost_ref[...])


def _ffn(x, gpre, wg, wu, wd, gpost, *, tm=512, tf=512):
    n, d = x.shape
    dff = wg.shape[1]
    tm = _tile(n, tm)
    return pl.pallas_call(
        _ffn_kernel,
        grid=(n // tm, dff // tf),
        in_specs=[
            pl.BlockSpec((tm, d), lambda i, j: (i, 0)),
            pl.BlockSpec((1, d), lambda i, j: (0, 0)),
            pl.BlockSpec((d, tf), lambda i, j: (0, j)),
            pl.BlockSpec((d, tf), lambda i, j: (0, j)),
            pl.BlockSpec((tf, d), lambda i, j: (j, 0)),
            pl.BlockSpec((1, d), lambda i, j: (0, 0)),
        ],
        out_specs=pl.BlockSpec((tm, d), lambda i, j: (i, 0)),
        out_shape=jax.ShapeDtypeStruct((n, d), F32),
        scratch_shapes=[pltpu.VMEM((tm, d), BF16), pltpu.VMEM((tm, d), F32)],
        compiler_params=_params("parallel", "arbitrary"),
        name="ffn",
    )(x, gpre, wg, wu, wd, gpost)


def _inproj_kernel(h_ref, g_ref, w_ref, o_ref, u_ref):
    @pl.when(pl.program_id(1) == 0)
    def _():
        u_ref[...] = _rms(h_ref[...], g_ref[...]).astype(BF16)

    o_ref[...] = _dot(u_ref[...], w_ref[...])


def _inproj(h, g, w, *, tm=1024, tn=512):
    n, d = h.shape
    cols = w.shape[1]
    tm = _tile(n, tm)
    return pl.pallas_call(
        _inproj_kernel,
        grid=(n // tm, cols // tn),
        in_specs=[
            pl.BlockSpec((tm, d), lambda i, j: (i, 0)),
            pl.BlockSpec((1, d), lambda i, j: (0, 0)),
            pl.BlockSpec((d, tn), lambda i, j: (0, j)),
        ],
        out_specs=pl.BlockSpec((tm, tn), lambda i, j: (i, j)),
        out_shape=jax.ShapeDtypeStruct((n, cols), F32),
        scratch_shapes=[pltpu.VMEM((tm, d), BF16)],
        compiler_params=_params("parallel", "arbitrary"),
        name="inproj",
    )(h, g, w)


def _rope_tile(t, c, s):
    return t * c + pltpu.roll(t, 64, 1) * s


def _kv_kernel(lat_ref, rope_ref, c_ref, s_ref, g_ref, wuk_ref, wuv_ref, kv_ref, k_ref, v_ref):
    latn = _rms(lat_ref[...], g_ref[...])
    rot = _rope_tile(rope_ref[...], c_ref[...], s_ref[...])
    kv_ref[:, :KV_LORA] = latn
    kv_ref[:, KV_LORA:] = rot[:, :ROPE_DIM]
    lb = latn.astype(BF16)
    kn = _dot(lb, wuk_ref[...])
    v_ref[...] = _dot(lb, wuv_ref[...]).astype(BF16)
    rotb = rot.astype(BF16)
    for h in range(MLA_HEADS):
        k_ref[:, h * 256:h * 256 + 128] = kn[:, h * 128:(h + 1) * 128].astype(BF16)
        k_ref[:, h * 256 + 128:(h + 1) * 256] = rotb


def _kv_side(proj, cos_t, sin_t, g, wuk, wuv, *, tm=256):
    n = proj.shape[0]
    hd = MLA_HEADS * V_DIM
    period = cos_t.shape[0] // tm
    return pl.pallas_call(
        _kv_kernel,
        grid=(n // tm,),
        in_specs=[
            pl.BlockSpec((tm, KV_LORA), lambda i: (i, COL_LAT // KV_LORA)),
            pl.BlockSpec((tm, 128), lambda i: (i, COL_ROPE // 128)),
            pl.BlockSpec((tm, 128), lambda i: (i % period, 0)),
            pl.BlockSpec((tm, 128), lambda i: (i % period, 0)),
            pl.BlockSpec((1, KV_LORA), lambda i: (0, 0)),
            pl.BlockSpec((KV_LORA, hd), lambda i: (0, 0)),
            pl.BlockSpec((KV_LORA, hd), lambda i: (0, 0)),
        ],
        out_specs=[
            pl.BlockSpec((tm, KV_LORA + ROPE_DIM), lambda i: (i, 0)),
            pl.BlockSpec((tm, 2 * hd), lambda i: (i, 0)),
            pl.BlockSpec((tm, hd), lambda i: (i, 0)),
        ],
        out_shape=[
            jax.ShapeDtypeStruct((n, KV_LORA + ROPE_DIM), F32),
            jax.ShapeDtypeStruct((n, 2 * hd), BF16),
            jax.ShapeDtypeStruct((n, hd), BF16),
        ],
        compiler_params=_params("parallel"),
        name="kv_side",
    )(proj, proj, cos_t, sin_t, g, wuk, wuv)


def _q_kernel(cq_ref, c_ref, s_ref, g_ref, wq_ref, q_ref):
    cqn = _rms(cq_ref[...], g_ref[...]).astype(BF16)
    qa = _dot(cqn, wq_ref[...]) * (MLA_SCALE * LOG2E)
    c = c_ref[...]
    s = s_ref[...]
    for h in range(MLA_HEADS):
        q_ref[:, h * 256:h * 256 + 128] = qa[:, h * 256:h * 256 + 128].astype(BF16)
        t = qa[:, h * 256 + 128:(h + 1) * 256]
        q_ref[:, h * 256 + 128:(h + 1) * 256] = _rope_tile(t, c, s).astype(BF16)


def _q_side(proj, cos_t, sin_t, g, wq, *, tm=256):
    n = proj.shape[0]
    cols = wq.shape[1]
    period = cos_t.shape[0] // tm
    return pl.pallas_call(
        _q_kernel,
        grid=(n // tm,),
        in_specs=[
            pl.BlockSpec((tm, Q_LORA), lambda i: (i, COL_CQ // Q_LORA)),
            pl.BlockSpec((tm, 128), lambda i: (i % period, 0)),
            pl.BlockSpec((tm, 128), lambda i: (i % period, 0)),
            pl.BlockSpec((1, Q_LORA), lambda i: (0, 0)),
            pl.BlockSpec((Q_LORA, cols), lambda i: (0, 0)),
        ],
        out_specs=pl.BlockSpec((tm, cols), lambda i: (i, 0)),
        out_shape=jax.ShapeDtypeStruct((n, cols), BF16),
        compiler_params=_params("parallel"),
        name="q_side",
    )(proj, cos_t, sin_t, g, wq)


def _qlat_kernel(q_ref, w_ref, o_ref):
    o_ref[...] = _dot(q_ref[:, :NOPE_DIM], w_ref[...]).astype(BF16)


def _q_absorb(q, wukt, *, tm=256):
    n_s = q.shape[0]
    return pl.pallas_call(
        _qlat_kernel,
        grid=(n_s // tm, MLA_HEADS),
        in_specs=[
            pl.BlockSpec((tm, 256), lambda i, h: (i, h)),
            pl.BlockSpec((None, NOPE_DIM, KV_LORA), lambda i, h: (h, 0, 0)),
        ],
        out_specs=pl.BlockSpec((tm, KV_LORA), lambda i, h: (i, h)),
        out_shape=jax.ShapeDtypeStruct((n_s, MLA_HEADS * KV_LORA), BF16),
        compiler_params=_params("parallel", "arbitrary"),
        name="q_absorb",
    )(q, wukt)


def _uv_kernel(lat_ref, w_ref, o_ref):
    o_ref[...] = _dot(lat_ref[...].astype(BF16), w_ref[...])


def _v_up(lat, wuvh, *, tm=256):
    n_s = lat.shape[0]
    return pl.pallas_call(
        _uv_kernel,
        grid=(n_s // tm, MLA_HEADS),
        in_specs=[
            pl.BlockSpec((tm, KV_LORA), lambda i, h: (i, h)),
            pl.BlockSpec((None, KV_LORA, V_DIM), lambda i, h: (h, 0, 0)),
        ],
        out_specs=pl.BlockSpec((tm, V_DIM), lambda i, h: (i, h)),
        out_shape=jax.ShapeDtypeStruct((n_s, MLA_HEADS * V_DIM), F32),
        compiler_params=_params("parallel", "arbitrary"),
        name="v_up",
    )(lat, wuvh)


def _softmax_step(s, v, m, l, acc):
    m_new = jnp.maximum(m, jnp.max(s, axis=-1, keepdims=True))
    alpha = jnp.exp2(m - m_new)
    p = jnp.exp2(s - m_new)
    l_new = alpha * l + jnp.sum(p, axis=-1, keepdims=True)
    acc_new = alpha * acc + _dot(p.astype(BF16), v)
    return m_new, l_new, acc_new


def _softmax_part(s):
    m = jnp.max(s, axis=-1, keepdims=True)
    p = jnp.exp2(s - m)
    return m, jnp.sum(p, axis=-1, keepdims=True), p.astype(BF16)


def _merge_parts(m_ref, l_ref, acc_ref, parts):
    m_old = m_ref[...]
    m_new = m_old
    for m, _, _ in parts:
        m_new = jnp.maximum(m_new, m)
    alpha = jnp.exp2(m_old - m_new)
    l = alpha * l_ref[...]
    acc = alpha * acc_ref[...]
    for m, lp, ap in parts:
        w = jnp.exp2(m - m_new)
        l = l + w * lp
        acc = acc + w * ap
    m_ref[...] = m_new
    l_ref[...] = l
    acc_ref[...] = acc


def _init_state(m_ref, l_ref, acc_ref):
    m_ref[...] = jnp.full_like(m_ref, -jnp.inf)
    l_ref[...] = jnp.zeros_like(l_ref)
    acc_ref[...] = jnp.zeros_like(acc_ref)


def _pmla_kernel(q_ref, k_ref, v_ref, o_ref, *, tq):
    qi = pl.program_id(2)
    q = q_ref[...]
    tk = tq // 2

    def span(off, state, masks=None):
        o0 = pl.multiple_of(off, tk)
        o1 = pl.multiple_of(off + tk, tk)
        s0 = _dot_nt(q, k_ref[pl.ds(o0, tk), :])
        s1 = _dot_nt(q, k_ref[pl.ds(o1, tk), :])
        if masks is not None:
            s0 = jnp.where(masks[0], s0, -jnp.inf)
            s1 = jnp.where(masks[1], s1, -jnp.inf)
        state = _softmax_step(s0, v_ref[pl.ds(o0, tk), :], *state)
        return _softmax_step(s1, v_ref[pl.ds(o1, tk), :], *state)

    init = (jnp.full((tq, 1), -jnp.inf, F32), jnp.zeros((tq, 1), F32), jnp.zeros((tq, V_DIM), F32))
    state = lax.fori_loop(0, qi, lambda i, st: span(i * tq, st), init)
    row = lax.broadcasted_iota(jnp.int32, (tq, tk), 0)
    col = lax.broadcasted_iota(jnp.int32, (tq, tk), 1)
    m, l, acc = span(qi * tq, state, (col <= row, col + tk <= row))
    o_ref[...] = acc / l


def _prompt_mla(q, k, v, batch, seq, *, tq=1024):
    tq = _tile(seq, tq)
    nq = seq // tq
    return pl.pallas_call(
        functools.partial(_pmla_kernel, tq=tq),
        grid=(batch, MLA_HEADS, nq),
        in_specs=[
            pl.BlockSpec((tq, 256), lambda b, h, i: (b * nq + i, h)),
            pl.BlockSpec((seq, 256), lambda b, h, i: (b, h)),
            pl.BlockSpec((seq, V_DIM), lambda b, h, i: (b, h)),
        ],
        out_specs=pl.BlockSpec((tq, V_DIM), lambda b, h, i: (b * nq + i, h)),
        out_shape=jax.ShapeDtypeStruct((batch * seq, MLA_HEADS * V_DIM), F32),
        compiler_params=_params("parallel", "parallel", "arbitrary"),
        name="prompt_mla",
    )(q, k, v)


def _lambda(lq1_ref, lk1_ref, lq2_ref, lk2_ref):
    a = jnp.sum(lq1_ref[...] * lk1_ref[...], axis=-1, keepdims=True)
    b = jnp.sum(lq2_ref[...] * lk2_ref[...], axis=-1, keepdims=True)
    return jnp.exp(a) - jnp.exp(b) + LAM_INIT


def _diff_finish(a1, l1, a2, l2, lam, g):
    out = a1 / l1 - lam * (a2 / l2)
    return _rms(out, g) * (1.0 - LAM_INIT)


def _slope2(head):
    return jnp.exp2(-(head + 1).astype(F32) * (8.0 / DIFF_HEADS)) * LOG2E


def _pdiff_kernel(q_ref, k_ref, v_ref, lq1_ref, lk1_ref, lq2_ref, lk2_ref, g_ref, o_ref, kb_ref, vb_ref, *, tq):
    h = pl.program_id(1)
    qi = pl.program_id(2)
    hd = DIFF_HEAD_DIM
    tk = tq // 2

    @pl.when((h == 0) & (qi == 0))
    def _():
        kb_ref[...] = k_ref[...].astype(BF16)
        vb_ref[...] = v_ref[...].astype(BF16)

    slope = _slope2(jnp.full((1, 1), h, jnp.int32))
    q = q_ref[...] * (DIFF_SCALE * LOG2E)
    qs = (q[:, :hd].astype(BF16), q[:, hd:].astype(BF16))

    def logits(c, off, mask):
        rel = lax.broadcasted_iota(jnp.int32, (1, tk), 1) + (off - qi * tq)
        s = _dot_nt(qs[c], kb_ref[pl.ds(off, tk), c * hd:(c + 1) * hd]) + slope * rel.astype(F32)
        return s if mask is None else jnp.where(mask, s, -jnp.inf)

    def span(off, states, masks=(None, None)):
        offs = (pl.multiple_of(off, tk), pl.multiple_of(off + tk, tk))
        s = [[logits(c, offs[j], masks[j]) for j in range(2)] for c in range(2)]
        for j in range(2):
            v = vb_ref[pl.ds(offs[j], tk), :]
            states = tuple(_softmax_step(s[c][j], v, *states[c]) for c in range(2))
        return states

    init = tuple((jnp.full((tq, 1), -jnp.inf, F32), jnp.zeros((tq, 1), F32), jnp.zeros((tq, 2 * hd), F32))
                 for _ in range(2))
    states = lax.fori_loop(0, qi, lambda i, st: span(i * tq, st), init)
    row = lax.broadcasted_iota(jnp.int32, (tq, tk), 0)
    col = lax.broadcasted_iota(jnp.int32, (tq, tk), 1)
    (_, l1, a1), (_, l2, a2) = span(qi * tq, states, (col <= row, col + tk <= row))
    lam = _lambda(lq1_ref, lk1_ref, lq2_ref, lk2_ref)
    o_ref[...] = _diff_finish(a1, l1, a2, l2, lam, g_ref[...])


def _prompt_diff(proj, lam_vecs, g, batch, seq, *, tq=1024):
    tq = _tile(seq, tq)
    nq = seq // tq
    w = 2 * DIFF_HEAD_DIM
    vec = pl.BlockSpec((1, DIFF_HEAD_DIM), lambda b, h, i: (0, 0))
    return pl.pallas_call(
        functools.partial(_pdiff_kernel, tq=tq),
        grid=(batch, DIFF_HEADS, nq),
        in_specs=[
            pl.BlockSpec((tq, w), lambda b, h, i: (b * nq + i, COL_DQ // w + h)),
            pl.BlockSpec((seq, w), lambda b, h, i: (b, COL_DK // w)),
            pl.BlockSpec((seq, w), lambda b, h, i: (b, COL_DV // w)),
            vec, vec, vec, vec,
            pl.BlockSpec((1, w), lambda b, h, i: (0, 0)),
        ],
        out_specs=pl.BlockSpec((tq, w), lambda b, h, i: (b * nq + i, h)),
        out_shape=jax.ShapeDtypeStruct((batch * seq, DIFF_HEADS * w), F32),
        scratch_shapes=[pltpu.VMEM((seq, w), BF16), pltpu.VMEM((seq, w), BF16)],
        compiler_params=_params("arbitrary", "arbitrary", "arbitrary"),
        name="prompt_diff",
    )(proj, proj, proj, *lam_vecs, g)


def _dmla_kernel(pt_ref, qlat_ref, qrot_ref, new_ref, *rest, pages, group):
    page_refs = rest[:pages]
    o_ref = rest[pages]
    m_ref, l_ref, acc_ref = rest[pages + 1:]
    p = pl.program_id(1)
    rows = qlat_ref.shape[0]

    @pl.when(p == 0)
    def _():
        _init_state(m_ref, l_ref, acc_ref)

    qlat = qlat_ref[...]
    qrot = qrot_ref[...]

    def attend(kt, mask=None):
        lat = kt[:KV_LORA]
        s = _dot(qlat, lat) + _dot(qrot, kt[KV_LORA:])
        if mask is not None:
            s = jnp.where(mask, s, -jnp.inf)
        m, l, pb = _softmax_part(s)
        return m, l, _dot_nt(pb, lat)

    parts = []
    for g0 in range(0, pages, group):
        kt = jnp.concatenate([page_refs[g0 + j][...].astype(BF16) for j in range(group)], axis=1)
        parts.append(attend(kt))
    _merge_parts(m_ref, l_ref, acc_ref, parts)

    @pl.when(p == pl.num_programs(1) - 1)
    def _():
        n_new = new_ref.shape[1]
        t = lax.broadcasted_iota(jnp.int32, (rows, n_new), 0) // MLA_HEADS
        c = lax.broadcasted_iota(jnp.int32, (rows, n_new), 1)
        _merge_parts(m_ref, l_ref, acc_ref, [attend(new_ref[...].astype(BF16), c <= t)])
        o_ref[...] = acc_ref[...] / l_ref[...]


def _decode_mla(page_table, qlat, qrot, kv_new_t, cache_t, *, pages=16, group=8):
    n_seq, rows, _ = qlat.shape
    n_pages = page_table.shape[1]
    width, page = cache_t.shape[2], cache_t.shape[3]
    pages = _tile(n_pages, pages)
    group = _tile(pages, group)
    steps = n_pages // pages

    def page_spec(kk):
        return pl.BlockSpec((None, None, width, page),
                            lambda n, p, pt: (0, pt[n * n_pages + p * pages + kk], 0, 0))

    grid_spec = pltpu.PrefetchScalarGridSpec(
        num_scalar_prefetch=1,
        grid=(n_seq, steps),
        in_specs=[
            pl.BlockSpec((None, rows, KV_LORA), lambda n, p, pt: (n, 0, 0)),
            pl.BlockSpec((None, rows, ROPE_DIM), lambda n, p, pt: (n, 0, 0)),
            pl.BlockSpec((None, width, page), lambda n, p, pt: (n, 0, 0)),
        ] + [page_spec(kk) for kk in range(pages)],
        out_specs=pl.BlockSpec((None, rows, KV_LORA), lambda n, p, pt: (n, 0, 0)),
        scratch_shapes=[
            pltpu.VMEM((rows, 1), F32),
            pltpu.VMEM((rows, 1), F32),
            pltpu.VMEM((rows, KV_LORA), F32),
        ],
    )
    return pl.pallas_call(
        functools.partial(_dmla_kernel, pages=pages, group=group),
        grid_spec=grid_spec,
        out_shape=jax.ShapeDtypeStruct((n_seq, rows, KV_LORA), F32),
        compiler_params=_params("parallel", "arbitrary"),
        name="decode_mla",
    )(page_table.reshape(-1), qlat, qrot, kv_new_t, *([cache_t] * pages))


def _ddiff_kernel(pt_ref, q_ref, knew_ref, vnew_ref, lq1_ref, lk1_ref, lq2_ref, lk2_ref, g_ref, *rest,
                  pages, group, page, past):
    k_refs = rest[:pages]
    v_refs = rest[pages:2 * pages]
    o_ref = rest[2 * pages]
    m_ref, l_ref, acc_ref = rest[2 * pages + 1:]
    p = pl.program_id(1)
    rows = q_ref.shape[0]
    half = rows // 2

    @pl.when(p == 0)
    def _():
        _init_state(m_ref, l_ref, acc_ref)

    q = (q_ref[...] * (DIFF_SCALE * LOG2E)).astype(BF16)
    r = lax.broadcasted_iota(jnp.int32, (rows, 1), 0)
    slope = _slope2(r % DIFF_HEADS)

    def attend(k, v, kpos0, mask=None):
        n_keys = k.shape[0]
        rel = lax.broadcasted_iota(jnp.int32, (1, n_keys), 1) + (kpos0 - past)
        s = _dot_nt(q, k) + slope * rel.astype(F32)
        if mask is not None:
            s = jnp.where(mask, s, -jnp.inf)
        m, l, pb = _softmax_part(s)
        return m, l, _dot(pb, v)

    parts = []
    for g0 in range(0, pages, group):
        k = jnp.concatenate([k_refs[g0 + j][...].astype(BF16) for j in range(group)], axis=0)
        v = jnp.concatenate([v_refs[g0 + j][...].astype(BF16) for j in range(group)], axis=0)
        parts.append(attend(k, v, (p * pages + g0) * page))
    _merge_parts(m_ref, l_ref, acc_ref, parts)

    @pl.when(p == pl.num_programs(1) - 1)
    def _():
        n_new = knew_ref.shape[0]
        t = (lax.broadcasted_iota(jnp.int32, (rows, n_new), 0) % half) // DIFF_HEADS
        c = lax.broadcasted_iota(jnp.int32, (rows, n_new), 1)
        new = attend(knew_ref[...].astype(BF16), vnew_ref[...].astype(BF16), past, c <= t)
        _merge_parts(m_ref, l_ref, acc_ref, [new])
        acc = acc_ref[...]
        l = l_ref[...]
        lam = _lambda(lq1_ref, lk1_ref, lq2_ref, lk2_ref)
        o_ref[...] = _diff_finish(acc[:half], l[:half], acc[half:], l[half:], lam, g_ref[...])


def _decode_diff(page_table, q_bd, k_new, v_new, lam_vecs, g, cache_k, cache_v, *, pages=16, group=8):
    n_seq, rows, w = q_bd.shape
    n_pages = page_table.shape[1]
    page = cache_k.shape[2]
    pages = _tile(n_pages, pages)
    group = _tile(pages, group)
    steps = n_pages // pages
    past = n_pages * page

    def page_spec(kk):
        return pl.BlockSpec((None, None, page, w),
                            lambda n, p, pt: (0, pt[n * n_pages + p * pages + kk], 0, 0))

    vec = pl.BlockSpec((1, DIFF_HEAD_DIM), lambda n, p, pt: (0, 0))
    grid_spec = pltpu.PrefetchScalarGridSpec(
        num_scalar_prefetch=1,
        grid=(n_seq, steps),
        in_specs=[
            pl.BlockSpec((None, rows, w), lambda n, p, pt: (n, 0, 0)),
            pl.BlockSpec((None, page, w), lambda n, p, pt: (n, 0, 0)),
            pl.BlockSpec((None, page, w), lambda n, p, pt: (n, 0, 0)),
            vec, vec, vec, vec,
            pl.BlockSpec((1, w), lambda n, p, pt: (0, 0)),
        ] + [page_spec(kk) for kk in range(pages)] * 2,
        out_specs=pl.BlockSpec((None, rows // 2, w), lambda n, p, pt: (n, 0, 0)),
        scratch_shapes=[
            pltpu.VMEM((rows, 1), F32),
            pltpu.VMEM((rows, 1), F32),
            pltpu.VMEM((rows, w), F32),
        ],
    )
    return pl.pallas_call(
        functools.partial(_ddiff_kernel, pages=pages, group=group, page=page, past=past),
        grid_spec=grid_spec,
        out_shape=jax.ShapeDtypeStruct((n_seq, rows // 2, w), F32),
        compiler_params=_params("parallel", "arbitrary"),
        name="decode_diff",
    )(page_table.reshape(-1), q_bd, k_new, v_new, *lam_vecs, g,
      *([cache_k] * pages), *([cache_v] * pages))


def _merge_kernel(g0_ref, g1_ref, om_ref, od_ref, h_ref, wo_ref, g_ref, o_ref):
    merged = jax.nn.sigmoid(g0_ref[...]) * om_ref[...] + jax.nn.sigmoid(g1_ref[...]) * od_ref[...]
    y = _dot(merged.astype(BF16), wo_ref[...])
    o_ref[...] = h_ref[...] + _rms(y, g_ref[...])


def _merge(proj, o_mla, o_diff, h, wo, g, *, tm=256):
    n, d = h.shape
    row = lambda c: pl.BlockSpec((tm, d), lambda i: (i, c))
    return pl.pallas_call(
        _merge_kernel,
        grid=(n // tm,),
        in_specs=[
            row(COL_GATE // d), row(COL_GATE // d + 1), row(0), row(0), row(0),
            pl.BlockSpec((d, d), lambda i: (0, 0)),
            pl.BlockSpec((1, d), lambda i: (0, 0)),
        ],
        out_specs=row(0),
        out_shape=jax.ShapeDtypeStruct((n, d), F32),
        compiler_params=_params("parallel"),
        name="merge",
    )(proj, proj, o_mla, o_diff, h, wo, g)


def _ple_kernel(h_ref, p_ref, wg_ref, wp_ref, g_ref, o_ref):
    h = h_ref[...]
    gate = jax.nn.sigmoid(_dot(h.astype(BF16), wg_ref[...]))
    emb = _dot(p_ref[...].astype(BF16), wp_ref[...])
    o_ref[...] = h + _rms(gate * emb, g_ref[...])


def _ple(h, p, wg, wp, g, *, tm=256):
    n, d = h.shape
    pd = p.shape[1]
    return pl.pallas_call(
        _ple_kernel,
        grid=(n // tm,),
        in_specs=[
            pl.BlockSpec((tm, d), lambda i: (i, 0)),
            pl.BlockSpec((tm, pd), lambda i: (i, 0)),
            pl.BlockSpec((d, d), lambda i: (0, 0)),
            pl.BlockSpec((pd, d), lambda i: (0, 0)),
            pl.BlockSpec((1, d), lambda i: (0, 0)),
        ],
        out_specs=pl.BlockSpec((tm, d), lambda i: (i, 0)),
        out_shape=jax.ShapeDtypeStruct((n, d), F32),
        compiler_params=_params("parallel"),
        name="ple",
    )(h, p, wg, wp, g)


def _swap_halves(w):
    half = w.shape[-1] // 2
    return jnp.concatenate([-w[..., half:], w[..., :half]], axis=-1)


def _rope_tables(pos):
    inv = 1.0 / (ROPE_BASE ** (jnp.arange(0, ROPE_DIM, 2, dtype=F32) / ROPE_DIM))
    ang = pos.astype(F32)[:, None] * inv[None, :]
    zeros = jnp.zeros((pos.shape[0], 128 - ROPE_DIM), F32)
    cos, sin = jnp.cos(ang), jnp.sin(ang)
    return (jnp.concatenate([cos, cos, zeros], axis=1), jnp.concatenate([sin, sin, zeros], axis=1))


def kernel(x_prompt, x_sample, cache_mla, cache_diff_k, cache_diff_v, page_table, p_prompt, p_sample,
           ffn1_pre_g, ffn1_wg, ffn1_wu, ffn1_wd, ffn1_post_g, mix_pre_g, w_in, q_a_norm_g, w_uq,
           kv_a_norm_g, w_uk, w_uv, lambda_q1, lambda_k1, lambda_q2, lambda_k2, diff_subln_g, w_o,
           mix_post_g, ffn2_pre_g, ffn2_wg, ffn2_wu, ffn2_wd, ffn2_post_g, w_ple, w_ple_gate, ple_post_g):
    assert ffn1_wg.shape[0] == 1, "single-layer stack expected"
    batch, seq, d = x_prompt.shape
    n_seq, t_new, _ = x_sample.shape
    n_p, n_s = batch * seq, n_seq * t_new
    past = page_table.shape[1] * cache_mla.shape[2]
    page = cache_mla.shape[2]

    xp, xs = x_prompt.reshape(n_p, d), x_sample.reshape(n_s, d)
    pe_p, pe_s = p_prompt[0].reshape(n_p, -1), p_sample[0].reshape(n_s, -1)
    tables_p = _rope_tables(jnp.arange(seq))
    tables_s = _rope_tables(jnp.tile(past + jnp.arange(t_new), 256 // t_new))

    w = w_in[0]
    o1 = Q_LORA
    o2 = o1 + KV_LORA
    o3 = o2 + ROPE_DIM
    o4 = o3 + DIFF_HEADS * 2 * DIFF_HEAD_DIM
    o5 = o4 + 2 * DIFF_HEAD_DIM
    o6 = o5 + 2 * DIFF_HEAD_DIM
    w_rope = w[:, o2:o3]
    w_all = jnp.concatenate([
        w[:, o6:], w[:, o3:o4], w[:, :o1], w[:, o1:o2], w[:, o4:o5], w[:, o5:o6],
        w_rope, _swap_halves(w_rope), jnp.zeros((d, PROJ_COLS - COL_ROPE - 128), F32)], axis=1).astype(BF16)

    uq = w_uq[0].reshape(Q_LORA, MLA_HEADS, NOPE_DIM + ROPE_DIM)
    uq_rope = uq[:, :, NOPE_DIM:]
    wq = jnp.concatenate([uq[:, :, :NOPE_DIM], uq_rope, _swap_halves(uq_rope)], axis=-1)
    wq = wq.reshape(Q_LORA, MLA_HEADS * 256).astype(BF16)
    wuk = w_uk[0].reshape(KV_LORA, MLA_HEADS * NOPE_DIM).astype(BF16)
    wukt = jnp.transpose(w_uk[0], (1, 2, 0)).astype(BF16)
    wuv = w_uv[0].reshape(KV_LORA, MLA_HEADS * V_DIM).astype(BF16)
    wuvh = jnp.transpose(w_uv[0], (1, 0, 2)).astype(BF16)
    lam_vecs = (lambda_q1, lambda_k1, lambda_q2, lambda_k2)
    bf = lambda a: a[0].astype(BF16)
    ffn1_w = (bf(ffn1_wg), bf(ffn1_wu), bf(ffn1_wd))
    ffn2_w = (bf(ffn2_wg), bf(ffn2_wu), bf(ffn2_wd))
    wo, wpg, wp = bf(w_o), bf(w_ple_gate), bf(w_ple)

    def front(x, tables):
        h1 = _ffn(x, ffn1_pre_g, *ffn1_w, ffn1_post_g)
        proj = _inproj(h1, mix_pre_g, w_all)
        kv_mla, k_up, v_up = _kv_side(proj, *tables, kv_a_norm_g, wuk, wuv)
        q = _q_side(proj, *tables, q_a_norm_g, wq)
        return h1, proj, kv_mla, k_up, v_up, q

    def back(proj, o_mla, o_diff, h1, pe):
        h2 = _merge(proj, o_mla, o_diff, h1, wo, mix_post_g)
        h3 = _ffn(h2, ffn2_pre_g, *ffn2_w, ffn2_post_g)
        return _ple(h3, pe, wpg, wp, ple_post_g)

    w_kv = 2 * DIFF_HEAD_DIM
    dkv = lambda proj: (proj[:, COL_DK:COL_DK + w_kv], proj[:, COL_DV:COL_DV + w_kv])

    h1_p, proj_p, kv_p, k_up, v_up, q_p = front(xp, tables_p)
    o_mla_p = _prompt_mla(q_p, k_up, v_up, batch, seq)
    o_diff_p = _prompt_diff(proj_p, lam_vecs, diff_subln_g, batch, seq)
    y_p = back(proj_p, o_mla_p, o_diff_p, h1_p, pe_p)
    dk_p, dv_p = dkv(proj_p)

    h1_s, proj_s, kv_s, _, _, q_s = front(xs, tables_s)
    dk_s, dv_s = dkv(proj_s)
    rows = t_new * MLA_HEADS
    qlat = _q_absorb(q_s, wukt).reshape(n_seq, rows, KV_LORA)
    qrot = q_s.reshape(n_s, MLA_HEADS, 256)[:, :, NOPE_DIM:NOPE_DIM + ROPE_DIM].reshape(n_seq, rows, ROPE_DIM)
    pad_new = lambda a: jnp.pad(a.reshape(n_seq, t_new, -1), ((0, 0), (0, page - t_new), (0, 0)))
    cache_t = jnp.swapaxes(cache_mla, 2, 3)
    kv_new_t = jnp.swapaxes(pad_new(kv_s), 1, 2)
    lat_s = _decode_mla(page_table, qlat, qrot, kv_new_t, cache_t)
    o_mla_s = _v_up(lat_s.reshape(n_s, MLA_HEADS * KV_LORA), wuvh)

    dq = proj_s[:, COL_DQ:COL_DQ + DIFF_HEADS * w_kv].reshape(n_seq, t_new * DIFF_HEADS, 2, DIFF_HEAD_DIM)
    zq = jnp.zeros_like(dq[:, :, 0])
    q_bd = jnp.concatenate([jnp.concatenate([dq[:, :, 0], zq], axis=-1),
                            jnp.concatenate([zq, dq[:, :, 1]], axis=-1)], axis=1)
    o_diff_s = _decode_diff(page_table, q_bd, pad_new(dk_s), pad_new(dv_s), lam_vecs,
                            diff_subln_g, cache_diff_k, cache_diff_v)
    y_s = back(proj_s, o_mla_s, o_diff_s.reshape(n_s, d), h1_s, pe_s)

    shape_p = lambda a: a.reshape(1, batch, seq, -1)
    shape_s = lambda a: a.reshape(1, n_seq, t_new, -1)
    return (y_p.reshape(batch, seq, d), y_s.reshape(n_seq, t_new, d),
            shape_p(kv_p), shape_p(dk_p), shape_p(dv_p), shape_s(kv_s), shape_s(dk_s), shape_s(dv_s))
```

```python
import functools
import math

import jax
import jax.numpy as jnp
from jax import lax
from jax.experimental import pallas as pl
from jax.experimental.pallas import tpu as pltpu

F32 = jnp.float32
BF16 = jnp.bfloat16

NORM_EPS = 1e-6
MLA_HEADS = 16
NOPE_DIM = 128
ROPE_DIM = 64
KV_LORA = 512
Q_LORA = 512
V_DIM = 128
MLA_SCALE = (NOPE_DIM + ROPE_DIM) ** -0.5
ROPE_BASE = 10000.0
DIFF_HEADS = 8
DIFF_HEAD_DIM = 128
DIFF_SCALE = DIFF_HEAD_DIM ** -0.5
LAM_INIT = 0.8 - 0.6 * math.exp(-0.3 * 0)
LOG2E = math.log2(math.e)

COL_GATE = 0
COL_DQ = 4096
COL_CQ = 6144
COL_LAT = 6656
COL_DK = 7168
COL_DV = 7424
COL_ROPE = 7680
PROJ_COLS = 8192

VMEM_LIMIT = 52 * 1024 * 1024


def _params(*sem):
    return pltpu.CompilerParams(dimension_semantics=sem, vmem_limit_bytes=VMEM_LIMIT)


def _tile(n, t):
    return math.gcd(n, t)


def _rms(x, g):
    return x * lax.rsqrt(jnp.mean(x * x, axis=-1, keepdims=True) + NORM_EPS) * g


def _dot(a, b):
    return jnp.dot(a, b, preferred_element_type=F32)


def _dot_nt(a, b):
    return lax.dot_general(a, b, (((1,), (1,)), ((), ())), preferred_element_type=F32)


def _ffn_kernel(x_ref, gpre_ref, wg_ref, wu_ref, wd_ref, gpost_ref, o_ref, xn_ref, acc_ref):
    j = pl.program_id(1)

    @pl.when(j == 0)
    def _():
        xn_ref[...] = _rms(x_ref[...], gpre_ref[...]).astype(BF16)
        acc_ref[...] = jnp.zeros_like(acc_ref)

    xn = xn_ref[...]
    g = _dot(xn, wg_ref[...])
    u = _dot(xn, wu_ref[...])
    mid = (g * jax.nn.sigmoid(g) * u).astype(BF16)
    acc_ref[...] += _dot(mid, wd_ref[...])

    @pl.when(j == pl.num_programs(1) - 1)
    def _():
        o_ref[...] = x_ref[...] + 0.5 * _rms(acc_ref[...], gpost_ref[...])


def _ffn(x, gpre, wg, wu, wd, gpost, *, tm=512, tf=512):
    n, d = x.shape
    dff = wg.shape[1]
    tm = _tile(n, tm)
    return pl.pallas_call(
        _ffn_kernel,
        grid=(n // tm, dff // tf),
        in_specs=[
            pl.BlockSpec((tm, d), lambda i, j: (i, 0)),
            pl.BlockSpec((1, d), lambda i, j: (0, 0)),
            pl.BlockSpec((d, tf), lambda i, j: (0, j)),
            pl.BlockSpec((d, tf), lambda i, j: (0, j)),
            pl.BlockSpec((tf, d), lambda i, j: (j, 0)),
            pl.BlockSpec((1, d), lambda i, j: (0, 0)),
        ],
        out_specs=pl.BlockSpec((tm, d), lambda i, j: (i, 0)),
        out_shape=jax.ShapeDtypeStruct((n, d), F32),
        scratch_shapes=[pltpu.VMEM((tm, d), BF16), pltpu.VMEM((tm, d), F32)],
        compiler_params=_params("parallel", "arbitrary"),
        name="ffn",
    )(x, gpre, wg, wu, wd, gpost)


def _inproj_kernel(h_ref, g_ref, w_ref, o_ref, u_ref):
    @pl.when(pl.program_id(1) == 0)
    def _():
        u_ref[...] = _rms(h_ref[...], g_ref[...]).astype(BF16)

    o_ref[...] = _dot(u_ref[...], w_ref[...])


def _inproj(h, g, w, *, tm=1024, tn=512):
    n, d = h.shape
    cols = w.shape[1]
    tm = _tile(n, tm)
    return pl.pallas_call(
        _inproj_kernel,
        grid=(n // tm, cols // tn),
        in_specs=[
            pl.BlockSpec((tm, d), lambda i, j: (i, 0)),
            pl.BlockSpec((1, d), lambda i, j: (0, 0)),
            pl.BlockSpec((d, tn), lambda i, j: (0, j)),
        ],
        out_specs=pl.BlockSpec((tm, tn), lambda i, j: (i, j)),
        out_shape=jax.ShapeDtypeStruct((n, cols), F32),
        scratch_shapes=[pltpu.VMEM((tm, d), BF16)],
        compiler_params=_params("parallel", "arbitrary"),
        name="inproj",
    )(h, g, w)


def _rope_tile(t, c, s):
    return t * c + pltpu.roll(t, 64, 1) * s


def _kv_kernel(lat_ref, rope_ref, c_ref, s_ref, g_ref, wuk_ref, wuv_ref, kv_ref, k_ref, v_ref):
    latn = _rms(lat_ref[...], g_ref[...])
    rot = _rope_tile(rope_ref[...], c_ref[...], s_ref[...])
    kv_ref[:, :KV_LORA] = latn
    kv_ref[:, KV_LORA:] = rot[:, :ROPE_DIM]
    lb = latn.astype(BF16)
    kn = _dot(lb, wuk_ref[...])
    v_ref[...] = _dot(lb, wuv_ref[...]).astype(BF16)
    rotb = rot.astype(BF16)
    for h in range(MLA_HEADS):
        k_ref[:, h * 256:h * 256 + 128] = kn[:, h * 128:(h + 1) * 128].astype(BF16)
        k_ref[:, h * 256 + 128:(h + 1) * 256] = rotb


def _kv_side(proj, cos_t, sin_t, g, wuk, wuv, *, tm=256):
    n = proj.shape[0]
    hd = MLA_HEADS * V_DIM
    period = cos_t.shape[0] // tm
    return pl.pallas_call(
        _kv_kernel,
        grid=(n // tm,),
        in_specs=[
            pl.BlockSpec((tm, KV_LORA), lambda i: (i, COL_LAT // KV_LORA)),
            pl.BlockSpec((tm, 128), lambda i: (i, COL_ROPE // 128)),
            pl.BlockSpec((tm, 128), lambda i: (i % period, 0)),
            pl.BlockSpec((tm, 128), lambda i: (i % period, 0)),
            pl.BlockSpec((1, KV_LORA), lambda i: (0, 0)),
            pl.BlockSpec((KV_LORA, hd), lambda i: (0, 0)),
            pl.BlockSpec((KV_LORA, hd), lambda i: (0, 0)),
        ],
        out_specs=[
            pl.BlockSpec((tm, KV_LORA + ROPE_DIM), lambda i: (i, 0)),
            pl.BlockSpec((tm, 2 * hd), lambda i: (i, 0)),
            pl.BlockSpec((tm, hd), lambda i: (i, 0)),
        ],
        out_shape=[
            jax.ShapeDtypeStruct((n, KV_LORA + ROPE_DIM), F32),
            jax.ShapeDtypeStruct((n, 2 * hd), BF16),
            jax.ShapeDtypeStruct((n, hd), BF16),
        ],
        compiler_params=_params("parallel"),
        name="kv_side",
    )(proj, proj, cos_t, sin_t, g, wuk, wuv)


def _q_kernel(cq_ref, c_ref, s_ref, g_ref, wq_ref, q_ref):
    cqn = _rms(cq_ref[...], g_ref[...]).astype(BF16)
    qa = _dot(cqn, wq_ref[...]) * (MLA_SCALE * LOG2E)
    c = c_ref[...]
    s = s_ref[...]
    for h in range(MLA_HEADS):
        q_ref[:, h * 256:h * 256 + 128] = qa[:, h * 256:h * 256 + 128].astype(BF16)
        t = qa[:, h * 256 + 128:(h + 1) * 256]
        q_ref[:, h * 256 + 128:(h + 1) * 256] = _rope_tile(t, c, s).astype(BF16)


def _q_side(proj, cos_t, sin_t, g, wq, *, tm=256):
    n = proj.shape[0]
    cols = wq.shape[1]
    period = cos_t.shape[0] // tm
    return pl.pallas_call(
        _q_kernel,
        grid=(n // tm,),
        in_specs=[
            pl.BlockSpec((tm, Q_LORA), lambda i: (i, COL_CQ // Q_LORA)),
            pl.BlockSpec((tm, 128), lambda i: (i % period, 0)),
            pl.BlockSpec((tm, 128), lambda i: (i % period, 0)),
            pl.BlockSpec((1, Q_LORA), lambda i: (0, 0)),
            pl.BlockSpec((Q_LORA, cols), lambda i: (0, 0)),
        ],
        out_specs=pl.BlockSpec((tm, cols), lambda i: (i, 0)),
        out_shape=jax.ShapeDtypeStruct((n, cols), BF16),
        compiler_params=_params("parallel"),
        name="q_side",
    )(proj, cos_t, sin_t, g, wq)


def _qlat_kernel(q_ref, w_ref, o_ref):
    o_ref[...] = _dot(q_ref[:, :NOPE_DIM], w_ref[...]).astype(BF16)


def _q_absorb(q, wukt, *, tm=256):
    n_s = q.shape[0]
    return pl.pallas_call(
        _qlat_kernel,
        grid=(n_s // tm, MLA_HEADS),
        in_specs=[
            pl.BlockSpec((tm, 256), lambda i, h: (i, h)),
            pl.BlockSpec((None, NOPE_DIM, KV_LORA), lambda i, h: (h, 0, 0)),
        ],
        out_specs=pl.BlockSpec((tm, KV_LORA), lambda i, h: (i, h)),
        out_shape=jax.ShapeDtypeStruct((n_s, MLA_HEADS * KV_LORA), BF16),
        compiler_params=_params("parallel", "arbitrary"),
        name="q_absorb",
    )(q, wukt)


def _uv_kernel(lat_ref, w_ref, o_ref):
    o_ref[...] = _dot(lat_ref[...].astype(BF16), w_ref[...])


def _v_up(lat, wuvh, *, tm=256):
    n_s = lat.shape[0]
    return pl.pallas_call(
        _uv_kernel,
        grid=(n_s // tm, MLA_HEADS),
        in_specs=[
            pl.BlockSpec((tm, KV_LORA), lambda i, h: (i, h)),
            pl.BlockSpec((None, KV_LORA, V_DIM), lambda i, h: (h, 0, 0)),
        ],
        out_specs=pl.BlockSpec((tm, V_DIM), lambda i, h: (i, h)),
        out_shape=jax.ShapeDtypeStruct((n_s, MLA_HEADS * V_DIM), F32),
        compiler_params=_params("parallel", "arbitrary"),
        name="v_up",
    )(lat, wuvh)


def _softmax_step(s, v, m, l, acc):
    m_new = jnp.maximum(m, jnp.max(s, axis=-1, keepdims=True))
    alpha = jnp.exp2(m - m_new)
    p = jnp.exp2(s - m_new)
    l_new = alpha * l + jnp.sum(p, axis=-1, keepdims=True)
    acc_new = alpha * acc + _dot(p.astype(BF16), v)
    return m_new, l_new, acc_new


def _softmax_part(s):
    m = jnp.max(s, axis=-1, keepdims=True)
    p = jnp.exp2(s - m)
    return m, jnp.sum(p, axis=-1, keepdims=True), p.astype(BF16)


def _merge_parts(m_ref, l_ref, acc_ref, parts):
    m_old = m_ref[...]
    m_new = m_old
    for m, _, _ in parts:
        m_new = jnp.maximum(m_new, m)
    alpha = jnp.exp2(m_old - m_new)
    l = alpha * l_ref[...]
    acc = alpha * acc_ref[...]
    for m, lp, ap in parts:
        w = jnp.exp2(m - m_new)
        l = l + w * lp
        acc = acc + w * ap
    m_ref[...] = m_new
    l_ref[...] = l
    acc_ref[...] = acc


def _init_state(m_ref, l_ref, acc_ref):
    m_ref[...] = jnp.full_like(m_ref, -jnp.inf)
    l_ref[...] = jnp.zeros_like(l_ref)
    acc_ref[...] = jnp.zeros_like(acc_ref)


def _pmla_kernel(q_ref, k_ref, v_ref, o_ref, *, tq):
    qi = pl.program_id(2)
    q = q_ref[...]
    tk = tq // 2

    def span(off, state, masks=None):
        o0 = pl.multiple_of(off, tk)
        o1 = pl.multiple_of(off + tk, tk)
        s0 = _dot_nt(q, k_ref[pl.ds(o0, tk), :])
        s1 = _dot_nt(q, k_ref[pl.ds(o1, tk), :])
        if masks is not None:
            s0 = jnp.where(masks[0], s0, -jnp.inf)
            s1 = jnp.where(masks[1], s1, -jnp.inf)
        state = _softmax_step(s0, v_ref[pl.ds(o0, tk), :], *state)
        return _softmax_step(s1, v_ref[pl.ds(o1, tk), :], *state)

    init = (jnp.full((tq, 1), -jnp.inf, F32), jnp.zeros((tq, 1), F32), jnp.zeros((tq, V_DIM), F32))
    state = lax.fori_loop(0, qi, lambda i, st: span(i * tq, st), init)
    row = lax.broadcasted_iota(jnp.int32, (tq, tk), 0)
    col = lax.broadcasted_iota(jnp.int32, (tq, tk), 1)
    m, l, acc = span(qi * tq, state, (col <= row, col + tk <= row))
    o_ref[...] = acc / l


def _prompt_mla(q, k, v, batch, seq, *, tq=1024):
    tq = _tile(seq, tq)
    nq = seq // tq
    return pl.pallas_call(
        functools.partial(_pmla_kernel, tq=tq),
        grid=(batch, MLA_HEADS, nq),
        in_specs=[
            pl.BlockSpec((tq, 256), lambda b, h, i: (b * nq + i, h)),
            pl.BlockSpec((seq, 256), lambda b, h, i: (b, h)),
            pl.BlockSpec((seq, V_DIM), lambda b, h, i: (b, h)),
        ],
        out_specs=pl.BlockSpec((tq, V_DIM), lambda b, h, i: (b * nq + i, h)),
        out_shape=jax.ShapeDtypeStruct((batch * seq, MLA_HEADS * V_DIM), F32),
        compiler_params=_params("parallel", "parallel", "arbitrary"),
        name="prompt_mla",
    )(q, k, v)


def _lambda(lq1_ref, lk1_ref, lq2_ref, lk2_ref):
    a = jnp.sum(lq1_ref[...] * lk1_ref[...], axis=-1, keepdims=True)
    b = jnp.sum(lq2_ref[...] * lk2_ref[...], axis=-1, keepdims=True)
    return jnp.exp(a) - jnp.exp(b) + LAM_INIT


def _diff_finish(a1, l1, a2, l2, lam, g):
    out = a1 / l1 - lam * (a2 / l2)
    return _rms(out, g) * (1.0 - LAM_INIT)


def _slope2(head):
    return jnp.exp2(-(head + 1).astype(F32) * (8.0 / DIFF_HEADS)) * LOG2E


def _pdiff_kernel(q_ref, k_ref, v_ref, lq1_ref, lk1_ref, lq2_ref, lk2_ref, g_ref, o_ref, kb_ref, vb_ref, *, tq):
    h = pl.program_id(1)
    qi = pl.program_id(2)
    hd = DIFF_HEAD_DIM
    tk = tq // 2

    @pl.when((h == 0) & (qi == 0))
    def _():
        kb_ref[...] = k_ref[...].astype(BF16)
        vb_ref[...] = v_ref[...].astype(BF16)

    slope = _slope2(jnp.full((1, 1), h, jnp.int32))
    q = q_ref[...] * (DIFF_SCALE * LOG2E)
    qs = (q[:, :hd].astype(BF16), q[:, hd:].astype(BF16))

    def logits(c, off, mask):
        rel = lax.broadcasted_iota(jnp.int32, (1, tk), 1) + (off - qi * tq)
        s = _dot_nt(qs[c], kb_ref[pl.ds(off, tk), c * hd:(c + 1) * hd]) + slope * rel.astype(F32)
        return s if mask is None else jnp.where(mask, s, -jnp.inf)

    def span(off, states, masks=(None, None)):
        offs = (pl.multiple_of(off, tk), pl.multiple_of(off + tk, tk))
        s = [[logits(c, offs[j], masks[j]) for j in range(2)] for c in range(2)]
        for j in range(2):
            v = vb_ref[pl.ds(offs[j], tk), :]
            states = tuple(_softmax_step(s[c][j], v, *states[c]) for c in range(2))
        return states

    init = tuple((jnp.full((tq, 1), -jnp.inf, F32), jnp.zeros((tq, 1), F32), jnp.zeros((tq, 2 * hd), F32))
                 for _ in range(2))
    states = lax.fori_loop(0, qi, lambda i, st: span(i * tq, st), init)
    row = lax.broadcasted_iota(jnp.int32, (tq, tk), 0)
    col = lax.broadcasted_iota(jnp.int32, (tq, tk), 1)
    (_, l1, a1), (_, l2, a2) = span(qi * tq, states, (col <= row, col + tk <= row))
    lam = _lambda(lq1_ref, lk1_ref, lq2_ref, lk2_ref)
    o_ref[...] = _diff_finish(a1, l1, a2, l2, lam, g_ref[...])


def _prompt_diff(proj, lam_vecs, g, batch, seq, *, tq=1024):
    tq = _tile(seq, tq)
    nq = seq // tq
    w = 2 * DIFF_HEAD_DIM
    vec = pl.BlockSpec((1, DIFF_HEAD_DIM), lambda b, h, i: (0, 0))
    return pl.pallas_call(
        functools.partial(_pdiff_kernel, tq=tq),
        grid=(batch, DIFF_HEADS, nq),
        in_specs=[
            pl.BlockSpec((tq, w), lambda b, h, i: (b * nq + i, COL_DQ // w + h)),
            pl.BlockSpec((seq, w), lambda b, h, i: (b, COL_DK // w)),
            pl.BlockSpec((seq, w), lambda b, h, i: (b, COL_DV // w)),
            vec, vec, vec, vec,
            pl.BlockSpec((1, w), lambda b, h, i: (0, 0)),
        ],
        out_specs=pl.BlockSpec((tq, w), lambda b, h, i: (b * nq + i, h)),
        out_shape=jax.ShapeDtypeStruct((batch * seq, DIFF_HEADS * w), F32),
        scratch_shapes=[pltpu.VMEM((seq, w), BF16), pltpu.VMEM((seq, w), BF16)],
        compiler_params=_params("arbitrary", "arbitrary", "arbitrary"),
        name="prompt_diff",
    )(proj, proj, proj, *lam_vecs, g)


def _page_copies(cache_ref, pt_ref, buf, sem, step, slot, pages):
    return [pltpu.make_async_copy(cache_ref.at[0, pt_ref[step * pages + kk]], buf.at[slot, kk], sem.at[slot])
            for kk in range(pages)]


def _paged_step(copies_of):
    steps = pl.num_programs(1)
    i = pl.program_id(0) * steps + pl.program_id(1)
    slot = i % 2

    @pl.when(i == 0)
    def _():
        for c in copies_of(0, 0):
            c.start()

    @pl.when(i + 1 < pl.num_programs(0) * steps)
    def _():
        for c in copies_of(i + 1, 1 - slot):
            c.start()

    for c in copies_of(i, slot):
        c.wait()
    return slot


def _dmla_kernel(pt_ref, qlat_ref, qrot_ref, new_ref, cache_ref, o_ref, buf, sem, m_ref, l_ref, acc_ref,
                 *, pages, group):
    p = pl.program_id(1)
    rows = qlat_ref.shape[0]
    slot = _paged_step(lambda step, sl: _page_copies(cache_ref, pt_ref, buf, sem, step, sl, pages))

    @pl.when(p == 0)
    def _():
        _init_state(m_ref, l_ref, acc_ref)

    qlat = qlat_ref[...]
    qrot = qrot_ref[...]

    def attend(kt, mask=None):
        lat = kt[:KV_LORA]
        s = _dot(qlat, lat) + _dot(qrot, kt[KV_LORA:])
        if mask is not None:
            s = jnp.where(mask, s, -jnp.inf)
        m, l, pb = _softmax_part(s)
        return m, l, _dot_nt(pb, lat)

    parts = []
    for g0 in range(0, pages, group):
        kt = jnp.concatenate([buf[slot, g0 + j].astype(BF16) for j in range(group)], axis=1)
        parts.append(attend(kt))
    _merge_parts(m_ref, l_ref, acc_ref, parts)

    @pl.when(p == pl.num_programs(1) - 1)
    def _():
        n_new = new_ref.shape[1]
        t = lax.broadcasted_iota(jnp.int32, (rows, n_new), 0) // MLA_HEADS
        c = lax.broadcasted_iota(jnp.int32, (rows, n_new), 1)
        _merge_parts(m_ref, l_ref, acc_ref, [attend(new_ref[...].astype(BF16), c <= t)])
        o_ref[...] = acc_ref[...] / l_ref[...]


def _decode_mla(page_table, qlat, qrot, kv_new_t, cache_t, *, pages=16, group=8):
    n_seq, rows, _ = qlat.shape
    n_pages = page_table.shape[1]
    width, page = cache_t.shape[2], cache_t.shape[3]
    pages = _tile(n_pages, pages)
    group = _tile(pages, group)
    steps = n_pages // pages

    grid_spec = pltpu.PrefetchScalarGridSpec(
        num_scalar_prefetch=1,
        grid=(n_seq, steps),
        in_specs=[
            pl.BlockSpec((None, rows, KV_LORA), lambda n, p, pt: (n, 0, 0)),
            pl.BlockSpec((None, rows, ROPE_DIM), lambda n, p, pt: (n, 0, 0)),
            pl.BlockSpec((None, width, page), lambda n, p, pt: (n, 0, 0)),
            pl.BlockSpec(memory_space=pl.ANY),
        ],
        out_specs=pl.BlockSpec((None, rows, KV_LORA), lambda n, p, pt: (n, 0, 0)),
        scratch_shapes=[
            pltpu.VMEM((2, pages, width, page), cache_t.dtype),
            pltpu.SemaphoreType.DMA((2,)),
            pltpu.VMEM((rows, 1), F32),
            pltpu.VMEM((rows, 1), F32),
            pltpu.VMEM((rows, KV_LORA), F32),
        ],
    )
    return pl.pallas_call(
        functools.partial(_dmla_kernel, pages=pages, group=group),
        grid_spec=grid_spec,
        out_shape=jax.ShapeDtypeStruct((n_seq, rows, KV_LORA), F32),
        compiler_params=_params("arbitrary", "arbitrary"),
        name="decode_mla",
    )(page_table.reshape(-1), qlat, qrot, kv_new_t, cache_t)


def _ddiff_kernel(pt_ref, q_ref, knew_ref, vnew_ref, lq1_ref, lk1_ref, lq2_ref, lk2_ref, g_ref,
                  ck_ref, cv_ref, o_ref, kbuf, vbuf, ksem, vsem, m_ref, l_ref, acc_ref, *, pages, group, page, past):
    p = pl.program_id(1)
    rows = q_ref.shape[0]
    half = rows // 2
    slot = _paged_step(lambda step, sl: (_page_copies(ck_ref, pt_ref, kbuf, ksem, step, sl, pages)
                                         + _page_copies(cv_ref, pt_ref, vbuf, vsem, step, sl, pages)))

    @pl.when(p == 0)
    def _():
        _init_state(m_ref, l_ref, acc_ref)

    q = (q_ref[...] * (DIFF_SCALE * LOG2E)).astype(BF16)
    r = lax.broadcasted_iota(jnp.int32, (rows, 1), 0)
    slope = _slope2(r % DIFF_HEADS)

    def attend(k, v, kpos0, mask=None):
        n_keys = k.shape[0]
        rel = lax.broadcasted_iota(jnp.int32, (1, n_keys), 1) + (kpos0 - past)
        s = _dot_nt(q, k) + slope * rel.astype(F32)
        if mask is not None:
            s = jnp.where(mask, s, -jnp.inf)
        m, l, pb = _softmax_part(s)
        return m, l, _dot(pb, v)

    parts = []
    for g0 in range(0, pages, group):
        k = jnp.concatenate([kbuf[slot, g0 + j].astype(BF16) for j in range(group)], axis=0)
        v = jnp.concatenate([vbuf[slot, g0 + j].astype(BF16) for j in range(group)], axis=0)
        parts.append(attend(k, v, (p * pages + g0) * page))
    _merge_parts(m_ref, l_ref, acc_ref, parts)

    @pl.when(p == pl.num_programs(1) - 1)
    def _():
        n_new = knew_ref.shape[0]
        t = (lax.broadcasted_iota(jnp.int32, (rows, n_new), 0) % half) // DIFF_HEADS
        c = lax.broadcasted_iota(jnp.int32, (rows, n_new), 1)
        new = attend(knew_ref[...].astype(BF16), vnew_ref[...].astype(BF16), past, c <= t)
        _merge_parts(m_ref, l_ref, acc_ref, [new])
        acc = acc_ref[...]
        l = l_ref[...]
        lam = _lambda(lq1_ref, lk1_ref, lq2_ref, lk2_ref)
        o_ref[...] = _diff_finish(acc[:half], l[:half], acc[half:], l[half:], lam, g_ref[...])


def _decode_diff(page_table, q_bd, k_new, v_new, lam_vecs, g, cache_k, cache_v, *, pages=16, group=8):
    n_seq, rows, w = q_bd.shape
    n_pages = page_table.shape[1]
    page = cache_k.shape[2]
    pages = _tile(n_pages, pages)
    group = _tile(pages, group)
    steps = n_pages // pages
    past = n_pages * page

    vec = pl.BlockSpec((1, DIFF_HEAD_DIM), lambda n, p, pt: (0, 0))
    grid_spec = pltpu.PrefetchScalarGridSpec(
        num_scalar_prefetch=1,
        grid=(n_seq, steps),
        in_specs=[
            pl.BlockSpec((None, rows, w), lambda n, p, pt: (n, 0, 0)),
            pl.BlockSpec((None, page, w), lambda n, p, pt: (n, 0, 0)),
            pl.BlockSpec((None, page, w), lambda n, p, pt: (n, 0, 0)),
            vec, vec, vec, vec,
            pl.BlockSpec((1, w), lambda n, p, pt: (0, 0)),
            pl.BlockSpec(memory_space=pl.ANY),
            pl.BlockSpec(memory_space=pl.ANY),
        ],
        out_specs=pl.BlockSpec((None, rows // 2, w), lambda n, p, pt: (n, 0, 0)),
        scratch_shapes=[
            pltpu.VMEM((2, pages, page, w), cache_k.dtype),
            pltpu.VMEM((2, pages, page, w), cache_v.dtype),
            pltpu.SemaphoreType.DMA((2,)),
            pltpu.SemaphoreType.DMA((2,)),
            pltpu.VMEM((rows, 1), F32),
            pltpu.VMEM((rows, 1), F32),
            pltpu.VMEM((rows, w), F32),
        ],
    )
    return pl.pallas_call(
        functools.partial(_ddiff_kernel, pages=pages, group=group, page=page, past=past),
        grid_spec=grid_spec,
        out_shape=jax.ShapeDtypeStruct((n_seq, rows // 2, w), F32),
        compiler_params=_params("arbitrary", "arbitrary"),
        name="decode_diff",
    )(page_table.reshape(-1), q_bd, k_new, v_new, *lam_vecs, g, cache_k, cache_v)


def _merge_kernel(g0_ref, g1_ref, om_ref, od_ref, h_ref, wo_ref, g_ref, o_ref):
    merged = jax.nn.sigmoid(g0_ref[...]) * om_ref[...] + jax.nn.sigmoid(g1_ref[...]) * od_ref[...]
    y = _dot(merged.astype(BF16), wo_ref[...])
    o_ref[...] = h_ref[...] + _rms(y, g_ref[...])


def _merge(proj, o_mla, o_diff, h, wo, g, *, tm=256):
    n, d = h.shape
    row = lambda c: pl.BlockSpec((tm, d), lambda i: (i, c))
    return pl.pallas_call(
        _merge_kernel,
        grid=(n // tm,),
        in_specs=[
            row(COL_GATE // d), row(COL_GATE // d + 1), row(0), row(0), row(0),
            pl.BlockSpec((d, d), lambda i: (0, 0)),
            pl.BlockSpec((1, d), lambda i: (0, 0)),
        ],
        out_specs=row(0),
        out_shape=jax.ShapeDtypeStruct((n, d), F32),
        compiler_params=_params("parallel"),
        name="merge",
    )(proj, proj, o_mla, o_diff, h, wo, g)


def _ple_kernel(h_ref, p_ref, wg_ref, wp_ref, g_ref, o_ref):
    h = h_ref[...]
    gate = jax.nn.sigmoid(_dot(h.astype(BF16), wg_ref[...]))
    emb = _dot(p_ref[...].astype(BF16), wp_ref[...])
    o_ref[...] = h + _rms(gate * emb, g_ref[...])


def _ple(h, p, wg, wp, g, *, tm=256):
    n, d = h.shape
    pd = p.shape[1]
    return pl.pallas_call(
        _ple_kernel,
        grid=(n // tm,),
        in_specs=[
            pl.BlockSpec((tm, d), lambda i: (i, 0)),
            pl.BlockSpec((tm, pd), lambda i: (i, 0)),
            pl.BlockSpec((d, d), lambda i: (0, 0)),
            pl.BlockSpec((pd, d), lambda i: (0, 0)),
            pl.BlockSpec((1, d), lambda i: (0, 0)),
        ],
        out_specs=pl.BlockSpec((tm, d), lambda i: (i, 0)),
        out_shape=jax.ShapeDtypeStruct((n, d), F32),
        compiler_params=_params("parallel"),
        name="ple",
    )(h, p, wg, wp, g)


def _swap_halves(w):
    half = w.shape[-1] // 2
    return jnp.concatenate([-w[..., half:], w[..., :half]], axis=-1)


def _rope_tables(pos):
    inv = 1.0 / (ROPE_BASE ** (jnp.arange(0, ROPE_DIM, 2, dtype=F32) / ROPE_DIM))
    ang = pos.astype(F32)[:, None] * inv[None, :]
    zeros = jnp.zeros((pos.shape[0], 128 - ROPE_DIM), F32)
    cos, sin = jnp.cos(ang), jnp.sin(ang)
    return (jnp.concatenate([cos, cos, zeros], axis=1), jnp.concatenate([sin, sin, zeros], axis=1))


def kernel(x_prompt, x_sample, cache_mla, cache_diff_k, cache_diff_v, page_table, p_prompt, p_sample,
           ffn1_pre_g, ffn1_wg, ffn1_wu, ffn1_wd, ffn1_post_g, mix_pre_g, w_in, q_a_norm_g, w_uq,
           kv_a_norm_g, w_uk, w_uv, lambda_q1, lambda_k1, lambda_q2, lambda_k2, diff_subln_g, w_o,
           mix_post_g, ffn2_pre_g, ffn2_wg, ffn2_wu, ffn2_wd, ffn2_post_g, w_ple, w_ple_gate, ple_post_g):
    assert ffn1_wg.shape[0] == 1, "single-layer stack expected"
    batch, seq, d = x_prompt.shape
    n_seq, t_new, _ = x_sample.shape
    n_p, n_s = batch * seq, n_seq * t_new
    past = page_table.shape[1] * cache_mla.shape[2]
    page = cache_mla.shape[2]

    xp, xs = x_prompt.reshape(n_p, d), x_sample.reshape(n_s, d)
    pe_p, pe_s = p_prompt[0].reshape(n_p, -1), p_sample[0].reshape(n_s, -1)
    tables_p = _rope_tables(jnp.arange(seq))
    tables_s = _rope_tables(jnp.tile(past + jnp.arange(t_new), 256 // t_new))

    w = w_in[0]
    o1 = Q_LORA
    o2 = o1 + KV_LORA
    o3 = o2 + ROPE_DIM
    o4 = o3 + DIFF_HEADS * 2 * DIFF_HEAD_DIM
    o5 = o4 + 2 * DIFF_HEAD_DIM
    o6 = o5 + 2 * DIFF_HEAD_DIM
    w_rope = w[:, o2:o3]
    w_all = jnp.concatenate([
        w[:, o6:], w[:, o3:o4], w[:, :o1], w[:, o1:o2], w[:, o4:o5], w[:, o5:o6],
        w_rope, _swap_halves(w_rope), jnp.zeros((d, PROJ_COLS - COL_ROPE - 128), F32)], axis=1).astype(BF16)

    uq = w_uq[0].reshape(Q_LORA, MLA_HEADS, NOPE_DIM + ROPE_DIM)
    uq_rope = uq[:, :, NOPE_DIM:]
    wq = jnp.concatenate([uq[:, :, :NOPE_DIM], uq_rope, _swap_halves(uq_rope)], axis=-1)
    wq = wq.reshape(Q_LORA, MLA_HEADS * 256).astype(BF16)
    wuk = w_uk[0].reshape(KV_LORA, MLA_HEADS * NOPE_DIM).astype(BF16)
    wukt = jnp.transpose(w_uk[0], (1, 2, 0)).astype(BF16)
    wuv = w_uv[0].reshape(KV_LORA, MLA_HEADS * V_DIM).astype(BF16)
    wuvh = jnp.transpose(w_uv[0], (1, 0, 2)).astype(BF16)
    lam_vecs = (lambda_q1, lambda_k1, lambda_q2, lambda_k2)
    bf = lambda a: a[0].astype(BF16)
    ffn1_w = (bf(ffn1_wg), bf(ffn1_wu), bf(ffn1_wd))
    ffn2_w = (bf(ffn2_wg), bf(ffn2_wu), bf(ffn2_wd))
    wo, wpg, wp = bf(w_o), bf(w_ple_gate), bf(w_ple)

    def front(x, tables):
        h1 = _ffn(x, ffn1_pre_g, *ffn1_w, ffn1_post_g)
        proj = _inproj(h1, mix_pre_g, w_all)
        kv_mla, k_up, v_up = _kv_side(proj, *tables, kv_a_norm_g, wuk, wuv)
        q = _q_side(proj, *tables, q_a_norm_g, wq)
        return h1, proj, kv_mla, k_up, v_up, q

    def back(proj, o_mla, o_diff, h1, pe):
        h2 = _merge(proj, o_mla, o_diff, h1, wo, mix_post_g)
        h3 = _ffn(h2, ffn2_pre_g, *ffn2_w, ffn2_post_g)
        return _ple(h3, pe, wpg, wp, ple_post_g)

    w_kv = 2 * DIFF_HEAD_DIM
    dkv = lambda proj: (proj[:, COL_DK:COL_DK + w_kv], proj[:, COL_DV:COL_DV + w_kv])

    h1_s, proj_s, kv_s, _, _, q_s = front(xs, tables_s)
    dk_s, dv_s = dkv(proj_s)
    rows = t_new * MLA_HEADS
    qlat = _q_absorb(q_s, wukt).reshape(n_seq, rows, KV_LORA)
    qrot = q_s.reshape(n_s, MLA_HEADS, 256)[:, :, NOPE_DIM:NOPE_DIM + ROPE_DIM].reshape(n_seq, rows, ROPE_DIM)
    pad_new = lambda a: jnp.pad(a.reshape(n_seq, t_new, -1), ((0, 0), (0, page - t_new), (0, 0)))
    cache_t = jnp.swapaxes(cache_mla, 2, 3)
    kv_new_t = jnp.swapaxes(pad_new(kv_s), 1, 2)
    lat_s = _decode_mla(page_table, qlat, qrot, kv_new_t, cache_t)
    o_mla_s = _v_up(lat_s.reshape(n_s, MLA_HEADS * KV_LORA), wuvh)

    dq = proj_s[:, COL_DQ:COL_DQ + DIFF_HEADS * w_kv].reshape(n_seq, t_new * DIFF_HEADS, 2, DIFF_HEAD_DIM)
    zq = jnp.zeros_like(dq[:, :, 0])
    q_bd = jnp.concatenate([jnp.concatenate([dq[:, :, 0], zq], axis=-1),
                            jnp.concatenate([zq, dq[:, :, 1]], axis=-1)], axis=1)
    o_diff_s = _decode_diff(page_table, q_bd, pad_new(dk_s), pad_new(dv_s), lam_vecs,
                            diff_subln_g, cache_diff_k, cache_diff_v)
    y_s = back(proj_s, o_mla_s, o_diff_s.reshape(n_s, d), h1_s, pe_s)

    h1_p, proj_p, kv_p, k_up, v_up, q_p = front(xp, tables_p)
    o_mla_p = _prompt_mla(q_p, k_up, v_up, batch, seq)
    o_diff_p = _prompt_diff(proj_p, lam_vecs, diff_subln_g, batch, seq)
    y_p = back(proj_p, o_mla_p, o_diff_p, h1_p, pe_p)
    dk_p, dv_p = dkv(proj_p)

    shape_p = lambda a: a.reshape(1, batch, seq, -1)
    shape_s = lambda a: a.reshape(1, n_seq, t_new, -1)
    return (y_p.reshape(batch, seq, d), y_s.reshape(n_seq, t_new, d),
            shape_p(kv_p), shape_p(dk_p), shape_p(dv_p), shape_s(kv_s), shape_s(dk_s), shape_s(dv_s))
```

```python
import functools
import math

import jax
import jax.numpy as jnp
from jax import lax
from jax.experimental import pallas as pl
from jax.experimental.pallas import tpu as pltpu

F32 = jnp.float32
BF16 = jnp.bfloat16

NORM_EPS = 1e-6
MLA_HEADS = 16
NOPE_DIM = 128
ROPE_DIM = 64
KV_LORA = 512
Q_LORA = 512
V_DIM = 128
MLA_SCALE = (NOPE_DIM + ROPE_DIM) ** -0.5
ROPE_BASE = 10000.0
DIFF_HEADS = 8
DIFF_HEAD_DIM = 128
DIFF_SCALE = DIFF_HEAD_DIM ** -0.5
LAM_INIT = 0.8 - 0.6 * math.exp(-0.3 * 0)
LOG2E = math.log2(math.e)

COL_GATE = 0
COL_DQ = 4096
COL_CQ = 6144
COL_LAT = 6656
COL_DK = 7168
COL_DV = 7424
COL_ROPE = 7680
PROJ_COLS = 8192

VMEM_LIMIT = 52 * 1024 * 1024


def _params(*sem):
    return pltpu.CompilerParams(dimension_semantics=sem, vmem_limit_bytes=VMEM_LIMIT)


def _tile(n, t):
    return math.gcd(n, t)


def _rms(x, g):
    return x * lax.rsqrt(jnp.mean(x * x, axis=-1, keepdims=True) + NORM_EPS) * g


def _dot(a, b):
    return jnp.dot(a, b, preferred_element_type=F32)


def _dot_nt(a, b):
    return lax.dot_general(a, b, (((1,), (1,)), ((), ())), preferred_element_type=F32)


def _ffn_kernel(x_ref, gpre_ref, wg_ref, wu_ref, wd_ref, gpost_ref, o_ref, xn_ref, acc_ref):
    j = pl.program_id(1)

    @pl.when(j == 0)
    def _():
        xn_ref[...] = _rms(x_ref[...], gpre_ref[...]).astype(BF16)
        acc_ref[...] = jnp.zeros_like(acc_ref)

    xn = xn_ref[...]
    g = _dot(xn, wg_ref[...])
    u = _dot(xn, wu_ref[...])
    mid = (g * jax.nn.sigmoid(g) * u).astype(BF16)
    acc_ref[...] += _dot(mid, wd_ref[...])

    @pl.when(j == pl.num_programs(1) - 1)
    def _():
        o_ref[...] = x_ref[...] + 0.5 * _rms(acc_ref[...], gpost_ref[...])


def _ffn(x, gpre, wg, wu, wd, gpost, *, tm=512, tf=512):
    n, d = x.shape
    dff = wg.shape[1]
    tm = _tile(n, tm)
    return pl.pallas_call(
        _ffn_kernel,
        grid=(n // tm, dff // tf),
        in_specs=[
            pl.BlockSpec((tm, d), lambda i, j: (i, 0)),
            pl.BlockSpec((1, d), lambda i, j: (0, 0)),
            pl.BlockSpec((d, tf), lambda i, j: (0, j)),
            pl.BlockSpec((d, tf), lambda i, j: (0, j)),
            pl.BlockSpec((tf, d), lambda i, j: (j, 0)),
            pl.BlockSpec((1, d), lambda i, j: (0, 0)),
        ],
        out_specs=pl.BlockSpec((tm, d), lambda i, j: (i, 0)),
        out_shape=jax.ShapeDtypeStruct((n, d), F32),
        scratch_shapes=[pltpu.VMEM((tm, d), BF16), pltpu.VMEM((tm, d), F32)],
        compiler_params=_params("parallel", "arbitrary"),
        name="ffn",
    )(x, gpre, wg, wu, wd, gpost)


def _inproj_kernel(h_ref, g_ref, w_ref, o_ref, u_ref):
    @pl.when(pl.program_id(1) == 0)
    def _():
        u_ref[...] = _rms(h_ref[...], g_ref[...]).astype(BF16)

    o_ref[...] = _dot(u_ref[...], w_ref[...])


def _inproj(h, g, w, *, tm=1024, tn=512):
    n, d = h.shape
    cols = w.shape[1]
    tm = _tile(n, tm)
    return pl.pallas_call(
        _inproj_kernel,
        grid=(n // tm, cols // tn),
        in_specs=[
            pl.BlockSpec((tm, d), lambda i, j: (i, 0)),
            pl.BlockSpec((1, d), lambda i, j: (0, 0)),
            pl.BlockSpec((d, tn), lambda i, j: (0, j)),
        ],
        out_specs=pl.BlockSpec((tm, tn), lambda i, j: (i, j)),
        out_shape=jax.ShapeDtypeStruct((n, cols), F32),
        scratch_shapes=[pltpu.VMEM((tm, d), BF16)],
        compiler_params=_params("parallel", "arbitrary"),
        name="inproj",
    )(h, g, w)


def _rope_tile(t, c, s):
    return t * c + pltpu.roll(t, 64, 1) * s


def _kv_kernel(lat_ref, rope_ref, c_ref, s_ref, g_ref, wuk_ref, wuv_ref, kv_ref, k_ref, v_ref):
    latn = _rms(lat_ref[...], g_ref[...])
    rot = _rope_tile(rope_ref[...], c_ref[...], s_ref[...])
    kv_ref[:, :KV_LORA] = latn
    kv_ref[:, KV_LORA:] = rot[:, :ROPE_DIM]
    lb = latn.astype(BF16)
    kn = _dot(lb, wuk_ref[...])
    v_ref[...] = _dot(lb, wuv_ref[...]).astype(BF16)
    rotb = rot.astype(BF16)
    for h in range(MLA_HEADS):
        k_ref[:, h * 256:h * 256 + 128] = kn[:, h * 128:(h + 1) * 128].astype(BF16)
        k_ref[:, h * 256 + 128:(h + 1) * 256] = rotb


def _kv_side(proj, cos_t, sin_t, g, wuk, wuv, *, tm=256):
    n = proj.shape[0]
    hd = MLA_HEADS * V_DIM
    period = cos_t.shape[0] // tm
    return pl.pallas_call(
        _kv_kernel,
        grid=(n // tm,),
        in_specs=[
            pl.BlockSpec((tm, KV_LORA), lambda i: (i, COL_LAT // KV_LORA)),
            pl.BlockSpec((tm, 128), lambda i: (i, COL_ROPE // 128)),
            pl.BlockSpec((tm, 128), lambda i: (i % period, 0)),
            pl.BlockSpec((tm, 128), lambda i: (i % period, 0)),
            pl.BlockSpec((1, KV_LORA), lambda i: (0, 0)),
            pl.BlockSpec((KV_LORA, hd), lambda i: (0, 0)),
            pl.BlockSpec((KV_LORA, hd), lambda i: (0, 0)),
        ],
        out_specs=[
            pl.BlockSpec((tm, KV_LORA + ROPE_DIM), lambda i: (i, 0)),
            pl.BlockSpec((tm, 2 * hd), lambda i: (i, 0)),
            pl.BlockSpec((tm, hd), lambda i: (i, 0)),
        ],
        out_shape=[
            jax.ShapeDtypeStruct((n, KV_LORA + ROPE_DIM), F32),
            jax.ShapeDtypeStruct((n, 2 * hd), BF16),
            jax.ShapeDtypeStruct((n, hd), BF16),
        ],
        compiler_params=_params("parallel"),
        name="kv_side",
    )(proj, proj, cos_t, sin_t, g, wuk, wuv)


def _q_kernel(cq_ref, c_ref, s_ref, g_ref, wq_ref, q_ref):
    cqn = _rms(cq_ref[...], g_ref[...]).astype(BF16)
    qa = _dot(cqn, wq_ref[...]) * (MLA_SCALE * LOG2E)
    c = c_ref[...]
    s = s_ref[...]
    for h in range(MLA_HEADS):
        q_ref[:, h * 256:h * 256 + 128] = qa[:, h * 256:h * 256 + 128].astype(BF16)
        t = qa[:, h * 256 + 128:(h + 1) * 256]
        q_ref[:, h * 256 + 128:(h + 1) * 256] = _rope_tile(t, c, s).astype(BF16)


def _q_side(proj, cos_t, sin_t, g, wq, *, tm=256):
    n = proj.shape[0]
    cols = wq.shape[1]
    period = cos_t.shape[0] // tm
    return pl.pallas_call(
        _q_kernel,
        grid=(n // tm,),
        in_specs=[
            pl.BlockSpec((tm, Q_LORA), lambda i: (i, COL_CQ // Q_LORA)),
            pl.BlockSpec((tm, 128), lambda i: (i % period, 0)),
            pl.BlockSpec((tm, 128), lambda i: (i % period, 0)),
            pl.BlockSpec((1, Q_LORA), lambda i: (0, 0)),
            pl.BlockSpec((Q_LORA, cols), lambda i: (0, 0)),
        ],
        out_specs=pl.BlockSpec((tm, cols), lambda i: (i, 0)),
        out_shape=jax.ShapeDtypeStruct((n, cols), BF16),
        compiler_params=_params("parallel"),
        name="q_side",
    )(proj, cos_t, sin_t, g, wq)


def _qlat_kernel(q_ref, w_ref, o_ref):
    o_ref[...] = _dot(q_ref[:, :NOPE_DIM], w_ref[...]).astype(BF16)


def _q_absorb(q, wukt, *, tm=256):
    n_s = q.shape[0]
    return pl.pallas_call(
        _qlat_kernel,
        grid=(n_s // tm, MLA_HEADS),
        in_specs=[
            pl.BlockSpec((tm, 256), lambda i, h: (i, h)),
            pl.BlockSpec((None, NOPE_DIM, KV_LORA), lambda i, h: (h, 0, 0)),
        ],
        out_specs=pl.BlockSpec((tm, KV_LORA), lambda i, h: (i, h)),
        out_shape=jax.ShapeDtypeStruct((n_s, MLA_HEADS * KV_LORA), BF16),
        compiler_params=_params("parallel", "arbitrary"),
        name="q_absorb",
    )(q, wukt)


def _uv_kernel(lat_ref, w_ref, o_ref):
    o_ref[...] = _dot(lat_ref[...].astype(BF16), w_ref[...])


def _v_up(lat, wuvh, *, tm=256):
    n_s = lat.shape[0]
    return pl.pallas_call(
        _uv_kernel,
        grid=(n_s // tm, MLA_HEADS),
        in_specs=[
            pl.BlockSpec((tm, KV_LORA), lambda i, h: (i, h)),
            pl.BlockSpec((None, KV_LORA, V_DIM), lambda i, h: (h, 0, 0)),
        ],
        out_specs=pl.BlockSpec((tm, V_DIM), lambda i, h: (i, h)),
        out_shape=jax.ShapeDtypeStruct((n_s, MLA_HEADS * V_DIM), F32),
        compiler_params=_params("parallel", "arbitrary"),
        name="v_up",
    )(lat, wuvh)


def _softmax_step(s, v, m, l, acc):
    m_new = jnp.maximum(m, jnp.max(s, axis=-1, keepdims=True))
    alpha = jnp.exp2(m - m_new)
    p = jnp.exp2(s - m_new)
    l_new = alpha * l + jnp.sum(p, axis=-1, keepdims=True)
    acc_new = alpha * acc + _dot(p.astype(BF16), v)
    return m_new, l_new, acc_new


def _softmax_part(s):
    m = jnp.max(s, axis=-1, keepdims=True)
    p = jnp.exp2(s - m)
    return m, jnp.sum(p, axis=-1, keepdims=True), p.astype(BF16)


def _merge_parts(m_ref, l_ref, acc_ref, parts):
    m_old = m_ref[...]
    m_new = m_old
    for m, _, _ in parts:
        m_new = jnp.maximum(m_new, m)
    alpha = jnp.exp2(m_old - m_new)
    l = alpha * l_ref[...]
    acc = alpha * acc_ref[...]
    for m, lp, ap in parts:
        w = jnp.exp2(m - m_new)
        l = l + w * lp
        acc = acc + w * ap
    m_ref[...] = m_new
    l_ref[...] = l
    acc_ref[...] = acc


def _init_state(m_ref, l_ref, acc_ref):
    m_ref[...] = jnp.full_like(m_ref, -jnp.inf)
    l_ref[...] = jnp.zeros_like(l_ref)
    acc_ref[...] = jnp.zeros_like(acc_ref)


def _pmla_kernel(q_ref, k_ref, v_ref, o_ref, *, tq):
    qi = pl.program_id(2)
    q = q_ref[...]
    tk = tq // 2

    def span(off, state, masks=None):
        o0 = pl.multiple_of(off, tk)
        o1 = pl.multiple_of(off + tk, tk)
        s0 = _dot_nt(q, k_ref[pl.ds(o0, tk), :])
        s1 = _dot_nt(q, k_ref[pl.ds(o1, tk), :])
        if masks is not None:
            s0 = jnp.where(masks[0], s0, -jnp.inf)
            s1 = jnp.where(masks[1], s1, -jnp.inf)
        state = _softmax_step(s0, v_ref[pl.ds(o0, tk), :], *state)
        return _softmax_step(s1, v_ref[pl.ds(o1, tk), :], *state)

    init = (jnp.full((tq, 1), -jnp.inf, F32), jnp.zeros((tq, 1), F32), jnp.zeros((tq, V_DIM), F32))
    state = lax.fori_loop(0, qi, lambda i, st: span(i * tq, st), init)
    row = lax.broadcasted_iota(jnp.int32, (tq, tk), 0)
    col = lax.broadcasted_iota(jnp.int32, (tq, tk), 1)
    m, l, acc = span(qi * tq, state, (col <= row, col + tk <= row))
    o_ref[...] = acc / l


def _prompt_mla(q, k, v, batch, seq, *, tq=1024):
    tq = _tile(seq, tq)
    nq = seq // tq
    return pl.pallas_call(
        functools.partial(_pmla_kernel, tq=tq),
        grid=(batch, MLA_HEADS, nq),
        in_specs=[
            pl.BlockSpec((tq, 256), lambda b, h, i: (b * nq + i, h)),
            pl.BlockSpec((seq, 256), lambda b, h, i: (b, h)),
            pl.BlockSpec((seq, V_DIM), lambda b, h, i: (b, h)),
        ],
        out_specs=pl.BlockSpec((tq, V_DIM), lambda b, h, i: (b * nq + i, h)),
        out_shape=jax.ShapeDtypeStruct((batch * seq, MLA_HEADS * V_DIM), F32),
        compiler_params=_params("parallel", "parallel", "arbitrary"),
        name="prompt_mla",
    )(q, k, v)


def _lambda(lq1_ref, lk1_ref, lq2_ref, lk2_ref):
    a = jnp.sum(lq1_ref[...] * lk1_ref[...], axis=-1, keepdims=True)
    b = jnp.sum(lq2_ref[...] * lk2_ref[...], axis=-1, keepdims=True)
    return jnp.exp(a) - jnp.exp(b) + LAM_INIT


def _diff_finish(a1, l1, a2, l2, lam, g):
    out = a1 / l1 - lam * (a2 / l2)
    return _rms(out, g) * (1.0 - LAM_INIT)


def _slope2(head):
    return jnp.exp2(-(head + 1).astype(F32) * (8.0 / DIFF_HEADS)) * LOG2E


def _pdiff_kernel(q_ref, k_ref, v_ref, lq1_ref, lk1_ref, lq2_ref, lk2_ref, g_ref, o_ref, kb_ref, vb_ref, *, tq):
    h = pl.program_id(1)
    qi = pl.program_id(2)
    hd = DIFF_HEAD_DIM
    tk = tq // 2

    @pl.when((h == 0) & (qi == 0))
    def _():
        kb_ref[...] = k_ref[...].astype(BF16)
        vb_ref[...] = v_ref[...].astype(BF16)

    slope = _slope2(jnp.full((1, 1), h, jnp.int32))
    q = q_ref[...] * (DIFF_SCALE * LOG2E)
    qs = (q[:, :hd].astype(BF16), q[:, hd:].astype(BF16))

    def logits(c, off, mask):
        rel = lax.broadcasted_iota(jnp.int32, (1, tk), 1) + (off - qi * tq)
        s = _dot_nt(qs[c], kb_ref[pl.ds(off, tk), c * hd:(c + 1) * hd]) + slope * rel.astype(F32)
        return s if mask is None else jnp.where(mask, s, -jnp.inf)

    def span(off, states, masks=(None, None)):
        offs = (pl.multiple_of(off, tk), pl.multiple_of(off + tk, tk))
        s = [[logits(c, offs[j], masks[j]) for j in range(2)] for c in range(2)]
        for j in range(2):
            v = vb_ref[pl.ds(offs[j], tk), :]
            states = tuple(_softmax_step(s[c][j], v, *states[c]) for c in range(2))
        return states

    init = tuple((jnp.full((tq, 1), -jnp.inf, F32), jnp.zeros((tq, 1), F32), jnp.zeros((tq, 2 * hd), F32))
                 for _ in range(2))
    states = lax.fori_loop(0, qi, lambda i, st: span(i * tq, st), init)
    row = lax.broadcasted_iota(jnp.int32, (tq, tk), 0)
    col = lax.broadcasted_iota(jnp.int32, (tq, tk), 1)
    (_, l1, a1), (_, l2, a2) = span(qi * tq, states, (col <= row, col + tk <= row))
    lam = _lambda(lq1_ref, lk1_ref, lq2_ref, lk2_ref)
    o_ref[...] = _diff_finish(a1, l1, a2, l2, lam, g_ref[...])


def _prompt_diff(proj, lam_vecs, g, batch, seq, *, tq=1024):
    tq = _tile(seq, tq)
    nq = seq // tq
    w = 2 * DIFF_HEAD_DIM
    vec = pl.BlockSpec((1, DIFF_HEAD_DIM), lambda b, h, i: (0, 0))
    return pl.pallas_call(
        functools.partial(_pdiff_kernel, tq=tq),
        grid=(batch, DIFF_HEADS, nq),
        in_specs=[
            pl.BlockSpec((tq, w), lambda b, h, i: (b * nq + i, COL_DQ // w + h)),
            pl.BlockSpec((seq, w), lambda b, h, i: (b, COL_DK // w)),
            pl.BlockSpec((seq, w), lambda b, h, i: (b, COL_DV // w)),
            vec, vec, vec, vec,
            pl.BlockSpec((1, w), lambda b, h, i: (0, 0)),
        ],
        out_specs=pl.BlockSpec((tq, w), lambda b, h, i: (b * nq + i, h)),
        out_shape=jax.ShapeDtypeStruct((batch * seq, DIFF_HEADS * w), F32),
        scratch_shapes=[pltpu.VMEM((seq, w), BF16), pltpu.VMEM((seq, w), BF16)],
        compiler_params=_params("arbitrary", "arbitrary", "arbitrary"),
        name="prompt_diff",
    )(proj, proj, proj, *lam_vecs, g)


def _page_copies(cache_ref, pt_ref, buf, sem, step, slot, pages):
    return [pltpu.make_async_copy(cache_ref.at[0, pt_ref[step * pages + kk]], buf.at[slot, kk], sem.at[slot])
            for kk in range(pages)]


def _paged_step(copies_of):
    steps = pl.num_programs(1)
    i = pl.program_id(0) * steps + pl.program_id(1)
    slot = i % 2

    def start_all(copies):
        for j, c in enumerate(copies):
            c.start(priority=j % 2)

    @pl.when(i == 0)
    def _():
        start_all(copies_of(0, 0))

    @pl.when(i + 1 < pl.num_programs(0) * steps)
    def _():
        start_all(copies_of(i + 1, 1 - slot))

    for c in copies_of(i, slot):
        c.wait()
    return slot


def _dmla_kernel(pt_ref, qlat_ref, qrot_ref, new_ref, cache_ref, o_ref, buf, sem, m_ref, l_ref, acc_ref,
                 *, pages, group):
    p = pl.program_id(1)
    rows = qlat_ref.shape[0]
    slot = _paged_step(lambda step, sl: _page_copies(cache_ref, pt_ref, buf, sem, step, sl, pages))

    @pl.when(p == 0)
    def _():
        _init_state(m_ref, l_ref, acc_ref)

    qlat = qlat_ref[...]
    qrot = qrot_ref[...]

    def attend(kt, mask=None):
        lat = kt[:KV_LORA]
        s = _dot(qlat, lat) + _dot(qrot, kt[KV_LORA:])
        if mask is not None:
            s = jnp.where(mask, s, -jnp.inf)
        m, l, pb = _softmax_part(s)
        return m, l, _dot_nt(pb, lat)

    parts = []
    for g0 in range(0, pages, group):
        kt = jnp.concatenate([buf[slot, g0 + j].astype(BF16) for j in range(group)], axis=1)
        parts.append(attend(kt))
    _merge_parts(m_ref, l_ref, acc_ref, parts)

    @pl.when(p == pl.num_programs(1) - 1)
    def _():
        n_new = new_ref.shape[1]
        t = lax.broadcasted_iota(jnp.int32, (rows, n_new), 0) // MLA_HEADS
        c = lax.broadcasted_iota(jnp.int32, (rows, n_new), 1)
        _merge_parts(m_ref, l_ref, acc_ref, [attend(new_ref[...].astype(BF16), c <= t)])
        o_ref[...] = acc_ref[...] / l_ref[...]


def _decode_mla(page_table, qlat, qrot, kv_new_t, cache_t, *, pages=16, group=8):
    n_seq, rows, _ = qlat.shape
    n_pages = page_table.shape[1]
    width, page = cache_t.shape[2], cache_t.shape[3]
    pages = _tile(n_pages, pages)
    group = _tile(pages, group)
    steps = n_pages // pages

    grid_spec = pltpu.PrefetchScalarGridSpec(
        num_scalar_prefetch=1,
        grid=(n_seq, steps),
        in_specs=[
            pl.BlockSpec((None, rows, KV_LORA), lambda n, p, pt: (n, 0, 0)),
            pl.BlockSpec((None, rows, ROPE_DIM), lambda n, p, pt: (n, 0, 0)),
            pl.BlockSpec((None, width, page), lambda n, p, pt: (n, 0, 0)),
            pl.BlockSpec(memory_space=pl.ANY),
        ],
        out_specs=pl.BlockSpec((None, rows, KV_LORA), lambda n, p, pt: (n, 0, 0)),
        scratch_shapes=[
            pltpu.VMEM((2, pages, width, page), cache_t.dtype),
            pltpu.SemaphoreType.DMA((2,)),
            pltpu.VMEM((rows, 1), F32),
            pltpu.VMEM((rows, 1), F32),
            pltpu.VMEM((rows, KV_LORA), F32),
        ],
    )
    return pl.pallas_call(
        functools.partial(_dmla_kernel, pages=pages, group=group),
        grid_spec=grid_spec,
        out_shape=jax.ShapeDtypeStruct((n_seq, rows, KV_LORA), F32),
        compiler_params=_params("arbitrary", "arbitrary"),
        name="decode_mla",
    )(page_table.reshape(-1), qlat, qrot, kv_new_t, cache_t)


def _ddiff_kernel(pt_ref, q_ref, knew_ref, vnew_ref, lq1_ref, lk1_ref, lq2_ref, lk2_ref, g_ref,
                  ck_ref, cv_ref, o_ref, kbuf, vbuf, ksem, vsem, m_ref, l_ref, acc_ref, *, pages, group, page, past):
    p = pl.program_id(1)
    rows = q_ref.shape[0]
    half = rows // 2
    slot = _paged_step(lambda step, sl: (_page_copies(ck_ref, pt_ref, kbuf, ksem, step, sl, pages)
                                         + _page_copies(cv_ref, pt_ref, vbuf, vsem, step, sl, pages)))

    @pl.when(p == 0)
    def _():
        _init_state(m_ref, l_ref, acc_ref)

    q = (q_ref[...] * (DIFF_SCALE * LOG2E)).astype(BF16)
    r = lax.broadcasted_iota(jnp.int32, (rows, 1), 0)
    slope = _slope2(r % DIFF_HEADS)

    def attend(k, v, kpos0, mask=None):
        n_keys = k.shape[0]
        rel = lax.broadcasted_iota(jnp.int32, (1, n_keys), 1) + (kpos0 - past)
        s = _dot_nt(q, k) + slope * rel.astype(F32)
        if mask is not None:
            s = jnp.where(mask, s, -jnp.inf)
        m, l, pb = _softmax_part(s)
        return m, l, _dot(pb, v)

    parts = []
    for g0 in range(0, pages, group):
        k = jnp.concatenate([kbuf[slot, g0 + j].astype(BF16) for j in range(group)], axis=0)
        v = jnp.concatenate([vbuf[slot, g0 + j].astype(BF16) for j in range(group)], axis=0)
        parts.append(attend(k, v, (p * pages + g0) * page))
    _merge_parts(m_ref, l_ref, acc_ref, parts)

    @pl.when(p == pl.num_programs(1) - 1)
    def _():
        n_new = knew_ref.shape[0]
        t = (lax.broadcasted_iota(jnp.int32, (rows, n_new), 0) % half) // DIFF_HEADS
        c = lax.broadcasted_iota(jnp.int32, (rows, n_new), 1)
        new = attend(knew_ref[...].astype(BF16), vnew_ref[...].astype(BF16), past, c <= t)
        _merge_parts(m_ref, l_ref, acc_ref, [new])
        acc = acc_ref[...]
        l = l_ref[...]
        lam = _lambda(lq1_ref, lk1_ref, lq2_ref, lk2_ref)
        o_ref[...] = _diff_finish(acc[:half], l[:half], acc[half:], l[half:], lam, g_ref[...])


def _decode_diff(page_table, q_bd, k_new, v_new, lam_vecs, g, cache_k, cache_v, *, pages=16, group=8):
    n_seq, rows, w = q_bd.shape
    n_pages = page_table.shape[1]
    page = cache_k.shape[2]
    pages = _tile(n_pages, pages)
    group = _tile(pages, group)
    steps = n_pages // pages
    past = n_pages * page

    vec = pl.BlockSpec((1, DIFF_HEAD_DIM), lambda n, p, pt: (0, 0))
    grid_spec = pltpu.PrefetchScalarGridSpec(
        num_scalar_prefetch=1,
        grid=(n_seq, steps),
        in_specs=[
            pl.BlockSpec((None, rows, w), lambda n, p, pt: (n, 0, 0)),
            pl.BlockSpec((None, page, w), lambda n, p, pt: (n, 0, 0)),
            pl.BlockSpec((None, page, w), lambda n, p, pt: (n, 0, 0)),
            vec, vec, vec, vec,
            pl.BlockSpec((1, w), lambda n, p, pt: (0, 0)),
            pl.BlockSpec(memory_space=pl.ANY),
            pl.BlockSpec(memory_space=pl.ANY),
        ],
        out_specs=pl.BlockSpec((None, rows // 2, w), lambda n, p, pt: (n, 0, 0)),
        scratch_shapes=[
            pltpu.VMEM((2, pages, page, w), cache_k.dtype),
            pltpu.VMEM((2, pages, page, w), cache_v.dtype),
            pltpu.SemaphoreType.DMA((2,)),
            pltpu.SemaphoreType.DMA((2,)),
            pltpu.VMEM((rows, 1), F32),
            pltpu.VMEM((rows, 1), F32),
            pltpu.VMEM((rows, w), F32),
        ],
    )
    return pl.pallas_call(
        functools.partial(_ddiff_kernel, pages=pages, group=group, page=page, past=past),
        grid_spec=grid_spec,
        out_shape=jax.ShapeDtypeStruct((n_seq, rows // 2, w), F32),
        compiler_params=_params("arbitrary", "arbitrary"),
        name="decode_diff",
    )(page_table.reshape(-1), q_bd, k_new, v_new, *lam_vecs, g, cache_k, cache_v)


def _merge_kernel(g0_ref, g1_ref, om_ref, od_ref, h_ref, wo_ref, g_ref, o_ref):
    merged = jax.nn.sigmoid(g0_ref[...]) * om_ref[...] + jax.nn.sigmoid(g1_ref[...]) * od_ref[...]
    y = _dot(merged.astype(BF16), wo_ref[...])
    o_ref[...] = h_ref[...] + _rms(y, g_ref[...])


def _merge(proj, o_mla, o_diff, h, wo, g, *, tm=256):
    n, d = h.shape
    row = lambda c: pl.BlockSpec((tm, d), lambda i: (i, c))
    return pl.pallas_call(
        _merge_kernel,
        grid=(n // tm,),
        in_specs=[
            row(COL_GATE // d), row(COL_GATE // d + 1), row(0), row(0), row(0),
            pl.BlockSpec((d, d), lambda i: (0, 0)),
            pl.BlockSpec((1, d), lambda i: (0, 0)),
        ],
        out_specs=row(0),
        out_shape=jax.ShapeDtypeStruct((n, d), F32),
        compiler_params=_params("parallel"),
        name="merge",
    )(proj, proj, o_mla, o_diff, h, wo, g)


def _ple_kernel(h_ref, p_ref, wg_ref, wp_ref, g_ref, o_ref):
    h = h_ref[...]
    gate = jax.nn.sigmoid(_dot(h.astype(BF16), wg_ref[...]))
    emb = _dot(p_ref[...].astype(BF16), wp_ref[...])
    o_ref[...] = h + _rms(gate * emb, g_ref[...])


def _ple(h, p, wg, wp, g, *, tm=256):
    n, d = h.shape
    pd = p.shape[1]
    return pl.pallas_call(
        _ple_kernel,
        grid=(n // tm,),
        in_specs=[
            pl.BlockSpec((tm, d), lambda i: (i, 0)),
            pl.BlockSpec((tm, pd), lambda i: (i, 0)),
            pl.BlockSpec((d, d), lambda i: (0, 0)),
            pl.BlockSpec((pd, d), lambda i: (0, 0)),
            pl.BlockSpec((1, d), lambda i: (0, 0)),
        ],
        out_specs=pl.BlockSpec((tm, d), lambda i: (i, 0)),
        out_shape=jax.ShapeDtypeStruct((n, d), F32),
        compiler_params=_params("parallel"),
        name="ple",
    )(h, p, wg, wp, g)


def _swap_halves(w):
    half = w.shape[-1] // 2
    return jnp.concatenate([-w[..., half:], w[..., :half]], axis=-1)


def _rope_tables(pos):
    inv = 1.0 / (ROPE_BASE ** (jnp.arange(0, ROPE_DIM, 2, dtype=F32) / ROPE_DIM))
    ang = pos.astype(F32)[:, None] * inv[None, :]
    zeros = jnp.zeros((pos.shape[0], 128 - ROPE_DIM), F32)
    cos, sin = jnp.cos(ang), jnp.sin(ang)
    return (jnp.concatenate([cos, cos, zeros], axis=1), jnp.concatenate([sin, sin, zeros], axis=1))


def kernel(x_prompt, x_sample, cache_mla, cache_diff_k, cache_diff_v, page_table, p_prompt, p_sample,
           ffn1_pre_g, ffn1_wg, ffn1_wu, ffn1_wd, ffn1_post_g, mix_pre_g, w_in, q_a_norm_g, w_uq,
           kv_a_norm_g, w_uk, w_uv, lambda_q1, lambda_k1, lambda_q2, lambda_k2, diff_subln_g, w_o,
           mix_post_g, ffn2_pre_g, ffn2_wg, ffn2_wu, ffn2_wd, ffn2_post_g, w_ple, w_ple_gate, ple_post_g):
    assert ffn1_wg.shape[0] == 1, "single-layer stack expected"
    batch, seq, d = x_prompt.shape
    n_seq, t_new, _ = x_sample.shape
    n_p, n_s = batch * seq, n_seq * t_new
    past = page_table.shape[1] * cache_mla.shape[2]
    page = cache_mla.shape[2]

    xp, xs = x_prompt.reshape(n_p, d), x_sample.reshape(n_s, d)
    pe_p, pe_s = p_prompt[0].reshape(n_p, -1), p_sample[0].reshape(n_s, -1)
    tables_p = _rope_tables(jnp.arange(seq))
    tables_s = _rope_tables(jnp.tile(past + jnp.arange(t_new), 256 // t_new))

    w = w_in[0]
    o1 = Q_LORA
    o2 = o1 + KV_LORA
    o3 = o2 + ROPE_DIM
    o4 = o3 + DIFF_HEADS * 2 * DIFF_HEAD_DIM
    o5 = o4 + 2 * DIFF_HEAD_DIM
    o6 = o5 + 2 * DIFF_HEAD_DIM
    w_rope = w[:, o2:o3]
    w_all = jnp.concatenate([
        w[:, o6:], w[:, o3:o4], w[:, :o1], w[:, o1:o2], w[:, o4:o5], w[:, o5:o6],
        w_rope, _swap_halves(w_rope), jnp.zeros((d, PROJ_COLS - COL_ROPE - 128), F32)], axis=1).astype(BF16)

    uq = w_uq[0].reshape(Q_LORA, MLA_HEADS, NOPE_DIM + ROPE_DIM)
    uq_rope = uq[:, :, NOPE_DIM:]
    wq = jnp.concatenate([uq[:, :, :NOPE_DIM], uq_rope, _swap_halves(uq_rope)], axis=-1)
    wq = wq.reshape(Q_LORA, MLA_HEADS * 256).astype(BF16)
    wuk = w_uk[0].reshape(KV_LORA, MLA_HEADS * NOPE_DIM).astype(BF16)
    wukt = jnp.transpose(w_uk[0], (1, 2, 0)).astype(BF16)
    wuv = w_uv[0].reshape(KV_LORA, MLA_HEADS * V_DIM).astype(BF16)
    wuvh = jnp.transpose(w_uv[0], (1, 0, 2)).astype(BF16)
    lam_vecs = (lambda_q1, lambda_k1, lambda_q2, lambda_k2)
    bf = lambda a: a[0].astype(BF16)
    ffn1_w = (bf(ffn1_wg), bf(ffn1_wu), bf(ffn1_wd))
    ffn2_w = (bf(ffn2_wg), bf(ffn2_wu), bf(ffn2_wd))
    wo, wpg, wp = bf(w_o), bf(w_ple_gate), bf(w_ple)

    def front(x, tables):
        h1 = _ffn(x, ffn1_pre_g, *ffn1_w, ffn1_post_g)
        proj = _inproj(h1, mix_pre_g, w_all)
        kv_mla, k_up, v_up = _kv_side(proj, *tables, kv_a_norm_g, wuk, wuv)
        q = _q_side(proj, *tables, q_a_norm_g, wq)
        return h1, proj, kv_mla, k_up, v_up, q

    def back(proj, o_mla, o_diff, h1, pe):
        h2 = _merge(proj, o_mla, o_diff, h1, wo, mix_post_g)
        h3 = _ffn(h2, ffn2_pre_g, *ffn2_w, ffn2_post_g)
        return _ple(h3, pe, wpg, wp, ple_post_g)

    w_kv = 2 * DIFF_HEAD_DIM
    dkv = lambda proj: (proj[:, COL_DK:COL_DK + w_kv], proj[:, COL_DV:COL_DV + w_kv])

    h1_s, proj_s, kv_s, _, _, q_s = front(xs, tables_s)
    dk_s, dv_s = dkv(proj_s)
    rows = t_new * MLA_HEADS
    qlat = _q_absorb(q_s, wukt).reshape(n_seq, rows, KV_LORA)
    qrot = q_s.reshape(n_s, MLA_HEADS, 256)[:, :, NOPE_DIM:NOPE_DIM + ROPE_DIM].reshape(n_seq, rows, ROPE_DIM)
    pad_new = lambda a: jnp.pad(a.reshape(n_seq, t_new, -1), ((0, 0), (0, page - t_new), (0, 0)))
    cache_t = jnp.swapaxes(cache_mla, 2, 3)
    kv_new_t = jnp.swapaxes(pad_new(kv_s), 1, 2)
    lat_s = _decode_mla(page_table, qlat, qrot, kv_new_t, cache_t)
    o_mla_s = _v_up(lat_s.reshape(n_s, MLA_HEADS * KV_LORA), wuvh)

    dq = proj_s[:, COL_DQ:COL_DQ + DIFF_HEADS * w_kv].reshape(n_seq, t_new * DIFF_HEADS, 2, DIFF_HEAD_DIM)
    zq = jnp.zeros_like(dq[:, :, 0])
    q_bd = jnp.concatenate([jnp.concatenate([dq[:, :, 0], zq], axis=-1),
                            jnp.concatenate([zq, dq[:, :, 1]], axis=-1)], axis=1)
    o_diff_s = _decode_diff(page_table, q_bd, pad_new(dk_s), pad_new(dv_s), lam_vecs,
                            diff_subln_g, cache_diff_k, cache_diff_v)
    y_s = back(proj_s, o_mla_s, o_diff_s.reshape(n_s, d), h1_s, pe_s)

    h1_p, proj_p, kv_p, k_up, v_up, q_p = front(xp, tables_p)
    o_mla_p = _prompt_mla(q_p, k_up, v_up, batch, seq)
    o_diff_p = _prompt_diff(proj_p, lam_vecs, diff_subln_g, batch, seq)
    y_p = back(proj_p, o_mla_p, o_diff_p, h1_p, pe_p)
    dk_p, dv_p = dkv(proj_p)

    shape_p = lambda a: a.reshape(1, batch, seq, -1)
    shape_s = lambda a: a.reshape(1, n_seq, t_new, -1)
    return (y_p.reshape(batch, seq, d), y_s.reshape(n_seq, t_new, d),
            shape_p(kv_p), shape_p(dk_p), shape_p(dv_p), shape_s(kv_s), shape_s(dk_s), shape_s(dv_s))
```
